```python
import jax, jax.numpy as jnp
from jax import lax
import numpy as np

D_MODEL = 1024
BATCH = 8
SEQ = 4096
DEPTH = 2
DEC_BATCH = 16
DEC_SEQ = 64
PAST_LEN = 1024

CHUNK = 64
LEFT_CHUNKS = 8
BAND_PAST = LEFT_CHUNKS * CHUNK
BAND = BAND_PAST + CHUNK
N_HEADS = 16
HEAD_DIM = D_MODEL // N_HEADS
REL_CLIP = 128
N_REL = 2 * REL_CLIP + 1
CONV_WIDTH = 31
CONV_STATE = CONV_WIDTH - 1
N_GROUPS = 4
EXPERTS_PER_GROUP = 8
N_EXPERTS = N_GROUPS * EXPERTS_PER_GROUP
TOP_K_IN_GROUP = 2
D_EXPERT = D_MODEL // 4
N_MIXERS = 2
N_ATTN_LAYERS = (DEPTH + 1) // 2
N_CONV_LAYERS = DEPTH // 2
RMS_EPS = 1e-6
LN_EPS = 1e-5
KV_KEEP = min(BAND_PAST, PAST_LEN)

kernel_name = 'streaming_chunkattn_conformerconv_hmoe'


def rms_norm(x, g):
    xf = x.astype(jnp.float32)
    y = xf * lax.rsqrt(jnp.mean(xf * xf, axis=-1, keepdims=True) + RMS_EPS)
    return (y * g.astype(jnp.float32)).astype(x.dtype)


def layer_norm(x, g, b):
    xf = x.astype(jnp.float32)
    mu = jnp.mean(xf, axis=-1, keepdims=True)
    xc = xf - mu
    y = xc * lax.rsqrt(jnp.mean(xc * xc, axis=-1, keepdims=True) + LN_EPS)
    return (y * g.astype(jnp.float32) + b.astype(jnp.float32)).astype(x.dtype)


def last_rows(x, n):
    t = x.shape[1]
    if t < n:
        x = jnp.pad(x, ((0, 0), (n - t, 0)) + ((0, 0),) * (x.ndim - 2))
    return x[:, x.shape[1] - n:]


def rel_bias(rel_table, q_pos, k_pos):
    d = jnp.clip(q_pos[:, None] - k_pos[None, :], -REL_CLIP, REL_CLIP) + REL_CLIP
    return rel_table.astype(jnp.float32)[:, d]


def qkv_proj(xn, w_qkv):
    b, t, _ = xn.shape
    qkv = (xn @ w_qkv).reshape(b, t, 3, N_HEADS, HEAD_DIM)
    return qkv[:, :, 0], qkv[:, :, 1], qkv[:, :, 2]


def band_attention_prompt(q, k, v, rel_table):
    b, t, h, dh = q.shape
    n_chunks = t // CHUNK
    pad = ((0, 0), (BAND_PAST, 0), (0, 0), (0, 0))
    k_pad = jnp.pad(k, pad)
    v_pad = jnp.pad(v, pad)
    q_c = q.reshape(b, n_chunks, CHUNK, h, dh).transpose(1, 0, 2, 3, 4)
    bias = rel_bias(rel_table, jnp.arange(CHUNK), jnp.arange(BAND) - BAND_PAST)
    scale = HEAD_DIM ** -0.5
    neg = jnp.finfo(jnp.float32).min

    def one_chunk(args):
        c, qc = args
        start = c * CHUNK
        kb = lax.dynamic_slice_in_dim(k_pad, start, BAND, axis=1)
        vb = lax.dynamic_slice_in_dim(v_pad, start, BAND, axis=1)
        s = jnp.einsum('bqhd,bkhd->bhqk', qc, kb).astype(jnp.float32) * scale + bias
        valid = (start - BAND_PAST + jnp.arange(BAND)) >= 0
        s = jnp.where(valid[None, None, None, :], s, neg)
        p = jax.nn.softmax(s, axis=-1).astype(vb.dtype)
        return jnp.einsum('bhqk,bkhd->bqhd', p, vb)

    out = lax.map(one_chunk, (jnp.arange(n_chunks), q_c))
    return out.transpose(1, 0, 2, 3, 4).reshape(b, t, h * dh)


def band_attention_sample(q, k_all, v_all, rel_table):
    b, tn, h, dh = q.shape
    q_pos = PAST_LEN + jnp.arange(tn)
    k_pos = PAST_LEN - KV_KEEP + jnp.arange(KV_KEEP + tn)
    bias = rel_bias(rel_table, q_pos, k_pos)
    s = jnp.einsum('bqhd,bkhd->bhqk', q, k_all).astype(jnp.float32) * (HEAD_DIM ** -0.5) + bias
    p = jax.nn.softmax(s, axis=-1).astype(v_all.dtype)
    return jnp.einsum('bhqk,bkhd->bqhd', p, v_all).reshape(b, tn, h * dh)


def conv_module(xn, conv_state, w_pw1, b_pw1, w_dw, b_dw, ln_g, ln_b, w_pw2, b_pw2):
    hid = xn @ w_pw1 + b_pw1
    a, g = jnp.split(hid, 2, axis=-1)
    u = a * jax.nn.sigmoid(g)
    u_ext = jnp.concatenate([conv_state.astype(u.dtype), u], axis=1)
    dw = lax.conv_general_dilated(
        u_ext, w_dw[:, None, :].astype(u.dtype), window_strides=(1,), padding='VALID',
        dimension_numbers=('NWC', 'WIO', 'NWC'), feature_group_count=D_MODEL) + b_dw
    z = jax.nn.silu(layer_norm(dw, ln_g, ln_b))
    return z @ w_pw2 + b_pw2, last_rows(u_ext, CONV_STATE)


def hier_moe(xn, w_group, b_group, w_router, b_router, w_gate, w_up, w_down):
    b, t, d = xn.shape
    xt = xn.reshape(b * t, d)
    g_probs = jax.nn.softmax((xt @ w_group).astype(jnp.float32) + b_group.astype(jnp.float32), axis=-1)
    g_w, g_idx = lax.top_k(g_probs, 1)
    e_logits = ((xt @ w_router).astype(jnp.float32) + b_router.astype(jnp.float32)).reshape(
        b * t, N_GROUPS, EXPERTS_PER_GROUP)
    in_group = jnp.take_along_axis(e_logits, g_idx[:, :, None], axis=1)[:, 0]
    top_v, top_i = lax.top_k(in_group, TOP_K_IN_GROUP)
    w2 = jax.nn.softmax(top_v, axis=-1)
    e_idx = g_idx * EXPERTS_PER_GROUP + top_i
    gates = jnp.einsum('nk,nke->ne', g_w * w2, jax.nn.one_hot(e_idx, N_EXPERTS, dtype=jnp.float32))
    y = jnp.zeros_like(xt)
    for grp in range(N_GROUPS):
        sl = slice(grp * EXPERTS_PER_GROUP, (grp + 1) * EXPERTS_PER_GROUP)
        hid = jax.nn.silu(jnp.einsum('nd,edf->nef', xt, w_gate[sl])) * jnp.einsum('nd,edf->nef', xt, w_up[sl])
        hid = hid * gates[:, sl, None].astype(hid.dtype)
        y = y + jnp.einsum('nef,efd->nd', hid, w_down[sl])
    return y.reshape(b, t, d)


def setup_inputs(seed: int = 0) -> dict:
    key = jax.random.key(seed)
    ks = jax.random.split(key, 26)
    f32 = jnp.float32
    D = D_MODEL
    A, C, L = N_ATTN_LAYERS, N_CONV_LAYERS, DEPTH

    def nrm(k, shape, scale):
        return jax.random.normal(k, shape, f32) * scale

    return {
        'x_prompt': nrm(ks[0], (BATCH, SEQ, D), 1.0),
        'x_sample': nrm(ks[1], (DEC_BATCH, DEC_SEQ, D), 1.0),
        'cache_attn_k': nrm(ks[2], (A, DEC_BATCH, KV_KEEP, N_HEADS, HEAD_DIM), 1.0),
        'cache_attn_v': nrm(ks[3], (A, DEC_BATCH, KV_KEEP, N_HEADS, HEAD_DIM), 1.0),
        'cache_conv': nrm(ks[4], (C, DEC_BATCH, CONV_STATE, D), 0.5),
        'norm_mix': 1.0 + nrm(ks[5], (L, D), 0.05),
        'norm_ffn': 1.0 + nrm(ks[6], (L, D), 0.05),
        'norm_final': 1.0 + nrm(ks[7], (D,), 0.05),
        'w_qkv': nrm(ks[8], (A, D, 3 * D), D ** -0.5),
        'w_o': nrm(ks[9], (A, D, D), D ** -0.5),
        'rel_table': nrm(ks[10], (A, N_HEADS, N_REL), 0.5),
        'w_pw1': nrm(ks[11], (C, D, 2 * D), D ** -0.5),
        'b_pw1': nrm(ks[12], (C, 2 * D), 0.02),
        'w_dw': nrm(ks[13], (C, CONV_WIDTH, D), CONV_WIDTH ** -0.5),
        'b_dw': nrm(ks[14], (C, D), 0.02),
        'ln_g': 1.0 + nrm(ks[15], (C, D), 0.05),
        'ln_b': nrm(ks[16], (C, D), 0.02),
        'w_pw2': nrm(ks[17], (C, D, D), D ** -0.5),
        'b_pw2': nrm(ks[18], (C, D), 0.02),
        'w_group': nrm(ks[19], (L, D, N_GROUPS), D ** -0.5),
        'b_group': nrm(ks[20], (L, N_GROUPS), 0.01),
        'w_router': nrm(ks[21], (L, D, N_EXPERTS), D ** -0.5),
        'b_router': nrm(ks[22], (L, N_EXPERTS), 0.01),
        'w_gate': nrm(ks[23], (L, N_EXPERTS, D, D_EXPERT), D ** -0.5),
        'w_up': nrm(ks[24], (L, N_EXPERTS, D, D_EXPERT), D ** -0.5),
        'w_down': nrm(ks[25], (L, N_EXPERTS, D_EXPERT, D), D_EXPERT ** -0.5),
    }


def reference(x_prompt, x_sample, cache_attn_k, cache_attn_v, cache_conv,
              norm_mix, norm_ffn, norm_final,
              w_qkv, w_o, rel_table,
              w_pw1, b_pw1, w_dw, b_dw, ln_g, ln_b, w_pw2, b_pw2,
              w_group, b_group, w_router, b_router, w_gate, w_up, w_down):
    xp, xs = x_prompt, x_sample
    kp_list, vp_list, cp_list = [], [], []
    ks_list, vs_list, cs_list = [], [], []
    for layer in range(DEPTH):
        hp = rms_norm(xp, norm_mix[layer])
        hs = rms_norm(xs, norm_mix[layer])
        if layer % N_MIXERS == 0:
            a = layer // N_MIXERS
            qp, kp, vp = qkv_proj(hp, w_qkv[a])
            op = band_attention_prompt(qp, kp, vp, rel_table[a])
            qs, kn, vn = qkv_proj(hs, w_qkv[a])
            k_all = jnp.concatenate([cache_attn_k[a].astype(kn.dtype), kn], axis=1)
            v_all = jnp.concatenate([cache_attn_v[a].astype(vn.dtype), vn], axis=1)
            os_ = band_attention_sample(qs, k_all, v_all, rel_table[a])
            xp = xp + op @ w_o[a]
            xs = xs + os_ @ w_o[a]
            kp_list.append(last_rows(kp, KV_KEEP))
            vp_list.append(last_rows(vp, KV_KEEP))
            ks_list.append(last_rows(k_all, KV_KEEP))
            vs_list.append(last_rows(v_all, KV_KEEP))
        else:
            c = layer // N_MIXERS
            zero_state = jnp.zeros((hp.shape[0], CONV_STATE, D_MODEL), hp.dtype)
            op, cp = conv_module(hp, zero_state, w_pw1[c], b_pw1[c], w_dw[c], b_dw[c],
                                 ln_g[c], ln_b[c], w_pw2[c], b_pw2[c])
            os_, cs = conv_module(hs, cache_conv[c], w_pw1[c], b_pw1[c], w_dw[c], b_dw[c],
                                  ln_g[c], ln_b[c], w_pw2[c], b_pw2[c])
            xp = xp + op
            xs = xs + os_
            cp_list.append(cp)
            cs_list.append(cs)
        xp = xp + hier_moe(rms_norm(xp, norm_ffn[layer]), w_group[layer], b_group[layer],
                           w_router[layer], b_router[layer], w_gate[layer], w_up[layer], w_down[layer])
        xs = xs + hier_moe(rms_norm(xs, norm_ffn[layer]), w_group[layer], b_group[layer],
                           w_router[layer], b_router[layer], w_gate[layer], w_up[layer], w_down[layer])
    y_prompt = rms_norm(xp, norm_final)
    y_sample = rms_norm(xs, norm_final)
    new_attn_k_prompt = jnp.stack(kp_list)
    new_attn_v_prompt = jnp.stack(vp_list)
    new_conv_prompt = jnp.stack(cp_list)
    new_attn_k_sample = jnp.stack(ks_list)
    new_attn_v_sample = jnp.stack(vs_list)
    new_conv_sample = jnp.stack(cs_list)
    return (y_prompt, y_sample, new_attn_k_prompt, new_attn_v_prompt, new_conv_prompt,
            new_attn_k_sample, new_attn_v_sample, new_conv_sample)
```

```python
import functools

import jax
import jax.numpy as jnp
from jax import lax
from jax.experimental import pallas as pl
from jax.experimental.pallas import tpu as pltpu

F32 = jnp.float32
BF16 = jnp.bfloat16

CHUNK = 64
LEFT_CHUNKS = 8
BAND_PAST = LEFT_CHUNKS * CHUNK
N_HEADS = 16
REL_CLIP = 128
N_GROUPS = 4
EXPERTS_PER_GROUP = 8
RMS_EPS = 1e-6
LN_EPS = 1e-5

LANES = 128
VMEM_LIMIT_BYTES = 56 * 1024 * 1024

MASK_VALUE = -1e30
ROW_TILE = 512
ATTN_Q_ROWS = 128
MOE_ROW_TILE = 256


def _params(*sem):
    return pltpu.CompilerParams(dimension_semantics=sem, vmem_limit_bytes=VMEM_LIMIT_BYTES)


def _rms(x, g):
    return x * lax.rsqrt(jnp.mean(x * x, axis=-1, keepdims=True) + RMS_EPS) * g


def _qkv_kernel(x_ref, g_ref, w_ref, q_ref, k_ref, v_ref, k32_ref, v32_ref, *, pad_blocks, d):
    j = pl.program_id(1)

    @pl.when(j < pad_blocks)
    def _():
        k_ref[...] = jnp.zeros_like(k_ref)
        v_ref[...] = jnp.zeros_like(v_ref)

    @pl.when(j >= pad_blocks)
    def _():
        xn = _rms(x_ref[0], g_ref[...]).astype(BF16)
        q = jnp.dot(xn, w_ref[:, 0:d], preferred_element_type=F32)
        q_ref[0] = (q * (float(d // N_HEADS) ** -0.5)).astype(BF16)
        k = jnp.dot(xn, w_ref[:, d:2 * d], preferred_element_type=F32)
        k_ref[0] = k.astype(BF16)
        k32_ref[0] = k
        v = jnp.dot(xn, w_ref[:, 2 * d:3 * d], preferred_element_type=F32)
        v_ref[0] = v.astype(BF16)
        v32_ref[0] = v


def _qkv(x, g, w, *, tm, pad_blocks):
    b, t, d = x.shape
    nt = t // tm
    xmap = lambda bi, j: (bi, jnp.maximum(j - pad_blocks, 0), 0)
    return pl.pallas_call(
        functools.partial(_qkv_kernel, pad_blocks=pad_blocks, d=d),
        grid=(b, nt + pad_blocks),
        in_specs=[
            pl.BlockSpec((1, tm, d), xmap),
            pl.BlockSpec((1, d), lambda bi, j: (0, 0)),
            pl.BlockSpec((d, 3 * d), lambda bi, j: (0, 0)),
        ],
        out_specs=[
            pl.BlockSpec((1, tm, d), xmap),
            pl.BlockSpec((1, tm, d), lambda bi, j: (bi, j, 0)),
            pl.BlockSpec((1, tm, d), lambda bi, j: (bi, j, 0)),
            pl.BlockSpec((1, tm, d), lambda bi, j: (bi, 0, 0)),
            pl.BlockSpec((1, tm, d), lambda bi, j: (bi, 0, 0)),
        ],
        out_shape=[
            jax.ShapeDtypeStruct((b, t, d), BF16),
            jax.ShapeDtypeStruct((b, t + pad_blocks * tm, d), BF16),
            jax.ShapeDtypeStruct((b, t + pad_blocks * tm, d), BF16),
            jax.ShapeDtypeStruct((b, tm, d), F32),
            jax.ShapeDtypeStruct((b, tm, d), F32),
        ],
        compiler_params=_params("arbitrary", "arbitrary"),
        name="qkv_proj",
    )(x, g, w)


def _attn_kernel(q_ref, kp_ref, kc_ref, vp_ref, vc_ref, bias_ref, o_ref, kbuf, vbuf,
                 *, tq, invalid_rows, n_heads):
    i = pl.program_id(1)
    tb = q_ref.shape[1]
    pad = kp_ref.shape[1]
    nk = tq + pad
    kbuf[0:pad, :] = kp_ref[0]
    kbuf[pad:pad + tb, :] = kc_ref[0]
    vbuf[0:pad, :] = vp_ref[0]
    vbuf[pad:pad + tb, :] = vc_ref[0]

    col = lax.broadcasted_iota(jnp.int32, (tq, nk), 1)
    lane = lax.broadcasted_iota(jnp.int32, (tq, LANES), 1)
    low_half = lane < (LANES // 2)

    def sub_tile(s, carry):
        r0 = pl.multiple_of(s * tq, tq)
        first_valid = invalid_rows - i * tb - r0
        pos_ok = col >= first_valid
        for hp in range(n_heads // 2):
            cols = slice(hp * LANES, (hp + 1) * LANES)
            q2 = q_ref[0, pl.ds(r0, tq), cols]
            k2 = kbuf[pl.ds(r0, nk), cols]
            v2 = vbuf[pl.ds(r0, nk), cols]
            outs = []
            for half in range(2):
                keep = low_half if half == 0 else jnp.logical_not(low_half)
                qm = jnp.where(keep, q2, jnp.zeros_like(q2))
                sc = lax.dot_general(qm, k2, (((1,), (1,)), ((), ())), preferred_element_type=F32)
                sc = jnp.where(pos_ok, sc + bias_ref[2 * hp + half], MASK_VALUE)
                m = jnp.max(sc, axis=-1, keepdims=True)
                p = jnp.exp(sc - m)
                l = jnp.sum(p, axis=-1, keepdims=True)
                o = jnp.dot(p.astype(BF16), v2, preferred_element_type=F32)
                outs.append(o / l)
            o_ref[0, pl.ds(r0, tq), cols] = jnp.where(low_half, outs[0], outs[1]).astype(o_ref.dtype)
        return carry

    lax.fori_loop(0, tb // tq, sub_tile, 0)


def _attention(q, kpad, vpad, bias, *, tb, tq, invalid_rows):
    b, t, d = q.shape
    pad = BAND_PAST
    cur0 = pad // tb
    return pl.pallas_call(
        functools.partial(_attn_kernel, tq=tq, invalid_rows=invalid_rows, n_heads=N_HEADS),
        grid=(b, t // tb),
        in_specs=[
            pl.BlockSpec((1, tb, d), lambda bi, i: (bi, i, 0)),
            pl.BlockSpec((1, pad, d), lambda bi, i: (bi, (i * tb) // pad, 0)),
            pl.BlockSpec((1, tb, d), lambda bi, i: (bi, cur0 + i, 0)),
            pl.BlockSpec((1, pad, d), lambda bi, i: (bi, (i * tb) // pad, 0)),
            pl.BlockSpec((1, tb, d), lambda bi, i: (bi, cur0 + i, 0)),
            pl.BlockSpec(bias.shape, lambda bi, i: (0, 0, 0)),
        ],
        out_specs=pl.BlockSpec((1, tb, d), lambda bi, i: (bi, i, 0)),
        out_shape=jax.ShapeDtypeStruct((b, t, d), BF16),
        scratch_shapes=[pltpu.VMEM((pad + tb, d), BF16), pltpu.VMEM((pad + tb, d), BF16)],
        compiler_params=_params("arbitrary", "arbitrary"),
        name="band_attention",
    )(q, kpad, kpad, vpad, vpad, bias)


def _band_bias(rel_table, tq):
    r = jnp.arange(tq)[:, None]
    c = jnp.arange(tq + BAND_PAST)[None, :]
    rel = jnp.clip(r - (c - BAND_PAST), -REL_CLIP, REL_CLIP) + REL_CLIP
    qc, kc = r // CHUNK, c // CHUNK
    in_band = (kc >= qc) & (kc <= qc + LEFT_CHUNKS)
    return jnp.where(in_band[None], rel_table.astype(F32)[:, rel], MASK_VALUE)


def _route(xn, wr_ref, br_ref):
    logits = jnp.dot(xn, wr_ref[...], preferred_element_type=F32,
                     precision=lax.Precision.HIGHEST) + br_ref[...]
    n = logits.shape[-1]
    lane = lax.broadcasted_iota(jnp.int32, logits.shape, 1).astype(F32)
    is_group = lane < N_GROUPS
    neg_inf = -jnp.inf
    far = float(n)
    gl = jnp.where(is_group, logits, neg_inf)
    gmax = jnp.max(gl, axis=-1, keepdims=True)
    gsum = jnp.sum(jnp.exp(gl - gmax), axis=-1, keepdims=True)
    g_w = 1.0 / gsum
    g_idx = jnp.min(jnp.where(gl == gmax, lane, far), axis=-1, keepdims=True)
    eidx = lane - N_GROUPS
    egroup = jnp.floor(eidx * (1.0 / EXPERTS_PER_GROUP))
    el = jnp.where(egroup == g_idx, logits, neg_inf)
    m1 = jnp.max(el, axis=-1, keepdims=True)
    i1 = jnp.min(jnp.where(el == m1, eidx, far), axis=-1, keepdims=True)
    el2 = jnp.where(eidx == i1, neg_inf, el)
    m2 = jnp.max(el2, axis=-1, keepdims=True)
    i2 = jnp.min(jnp.where(el2 == m2, eidx, far), axis=-1, keepdims=True)
    t = jnp.exp(m2 - m1)
    w_a = 1.0 / (1.0 + t)
    w_b = t * w_a
    return i1.astype(jnp.int32), i2.astype(jnp.int32), g_w * w_a, g_w * w_b


def _ffn_prologue(x1, gf_ref, wr_ref, br_ref, x1_ref, xn_ref, e1_ref, e2_ref, g1_ref, g2_ref):
    xn = _rms(x1, gf_ref[...])
    x1_ref[...] = x1.reshape(x1_ref.shape)
    xn_ref[...] = xn.astype(BF16).reshape(xn_ref.shape)
    e1, e2, g1, g2 = _route(xn, wr_ref, br_ref)
    e1_ref[...] = e1.reshape(e1_ref.shape)
    e2_ref[...] = e2.reshape(e2_ref.shape)
    g1_ref[...] = g1.reshape(g1_ref.shape)
    g2_ref[...] = g2.reshape(g2_ref.shape)


def _ffn_out_shapes(n, d):
    return [
        jax.ShapeDtypeStruct((n, d), F32),
        jax.ShapeDtypeStruct((n, d), BF16),
        jax.ShapeDtypeStruct((n, 1), jnp.int32),
        jax.ShapeDtypeStruct((n, 1), jnp.int32),
        jax.ShapeDtypeStruct((n, 1), F32),
        jax.ShapeDtypeStruct((n, 1), F32),
    ]


def _attn_out_kernel(a_ref, x_ref, wo_ref, gf_ref, wr_ref, br_ref, *out_refs):
    x1 = x_ref[...] + jnp.dot(a_ref[...], wo_ref[...], preferred_element_type=F32)
    _ffn_prologue(x1, gf_ref, wr_ref, br_ref, *out_refs)


def _attn_out(a, x, wo, gf, wr, br, *, tm):
    n, d = x.shape
    row = lambda i: (i, 0)
    fixed = lambda i: (0, 0)
    return pl.pallas_call(
        _attn_out_kernel,
        grid=(n // tm,),
        in_specs=[
            pl.BlockSpec((tm, d), row), pl.BlockSpec((tm, d), row),
            pl.BlockSpec(wo.shape, fixed), pl.BlockSpec(gf.shape, fixed),
            pl.BlockSpec(wr.shape, fixed), pl.BlockSpec(br.shape, fixed),
        ],
        out_specs=[pl.BlockSpec((tm, d), row), pl.BlockSpec((tm, d), row)]
        + [pl.BlockSpec((tm, 1), row)] * 4,
        out_shape=_ffn_out_shapes(n, d),
        compiler_params=_params("arbitrary"),
        name="attn_out_router",
    )(a, x, wo, gf, wr, br)


CONV_HEAD = 32


def _conv_kernel(x_ref, st_ref, gm_ref, w1_ref, b1_ref, wdw_ref, bdw_ref, lg_ref, lb_ref, w2_ref, b2_ref,
                 gf_ref, wr_ref, br_ref,
                 x1_ref, xn_ref, e1_ref, e2_ref, g1_ref, g2_ref, ns_ref, ubuf, *, width):
    t = pl.program_id(1)
    tt = x_ref.shape[1]
    d = x_ref.shape[2]
    state = width - 1
    lo = CONV_HEAD - state

    @pl.when(t == 0)
    def _():
        ubuf[lo:CONV_HEAD, :] = st_ref[0]

    x = x_ref[0]
    xn = _rms(x, gm_ref[...]).astype(BF16)
    a = jnp.dot(xn, w1_ref[:, 0:d], preferred_element_type=F32) + b1_ref[:, 0:d]
    g = jnp.dot(xn, w1_ref[:, d:2 * d], preferred_element_type=F32) + b1_ref[:, d:2 * d]
    ubuf[CONV_HEAD:CONV_HEAD + tt, :] = a * jax.nn.sigmoid(g)

    acc = jnp.zeros((tt, d), F32) + bdw_ref[...]
    for j in range(width):
        acc = acc + ubuf[lo + j:lo + j + tt, :] * wdw_ref[j:j + 1, :]

    ns_ref[0] = ubuf[CONV_HEAD + tt - state:CONV_HEAD + tt, :]
    ubuf[lo:CONV_HEAD, :] = ubuf[lo + tt:CONV_HEAD + tt, :]

    mu = jnp.mean(acc, axis=-1, keepdims=True)
    xc = acc - mu
    z = xc * lax.rsqrt(jnp.mean(xc * xc, axis=-1, keepdims=True) + LN_EPS) * lg_ref[...] + lb_ref[...]
    z = z * jax.nn.sigmoid(z)
    x1 = x + jnp.dot(z.astype(BF16), w2_ref[...], preferred_element_type=F32) + b2_ref[...]
    _ffn_prologue(x1, gf_ref, wr_ref, br_ref, x1_ref, xn_ref, e1_ref, e2_ref, g1_ref, g2_ref)


def _conv_layer(x, state, gm, w1, b1, wdw, bdw, lg, lb, w2, b2, gf, wr, br, *, tt):
    b, t, d = x.shape
    width = wdw.shape[0]
    nt = t // tt
    fixed = lambda bi, ti: (0, 0)
    tok = lambda bi, ti: (bi * nt + ti, 0)
    outs = pl.pallas_call(
        functools.partial(_conv_kernel, width=width),
        grid=(b, nt),
        in_specs=[
            pl.BlockSpec((1, tt, d), lambda bi, ti: (bi, ti, 0)),
            pl.BlockSpec((1, width - 1, d), lambda bi, ti: (bi, 0, 0)),
            pl.BlockSpec(gm.shape, fixed), pl.BlockSpec(w1.shape, fixed), pl.BlockSpec(b1.shape, fixed),
            pl.BlockSpec(wdw.shape, fixed), pl.BlockSpec(bdw.shape, fixed),
            pl.BlockSpec(lg.shape, fixed), pl.BlockSpec(lb.shape, fixed),
            pl.BlockSpec(w2.shape, fixed), pl.BlockSpec(b2.shape, fixed),
            pl.BlockSpec(gf.shape, fixed), pl.BlockSpec(wr.shape, fixed), pl.BlockSpec(br.shape, fixed),
        ],
        out_specs=[pl.BlockSpec((tt, d), tok), pl.BlockSpec((tt, d), tok)]
        + [pl.BlockSpec((tt, 1), tok)] * 4
        + [pl.BlockSpec((1, width - 1, d), lambda bi, ti: (bi, 0, 0))],
        out_shape=_ffn_out_shapes(b * t, d) + [jax.ShapeDtypeStruct((b, width - 1, d), F32)],
        scratch_shapes=[pltpu.VMEM((CONV_HEAD + tt, d), F32)],
        compiler_params=_params("arbitrary", "arbitrary"),
        name="conv_module_router",
    )(x, state, gm, w1, b1, wdw, bdw, lg, lb, w2, b2, gf, wr, br)
    return outs


def _moe_kernel(te_ref, nv_ref, x_ref, gate_ref, wg_ref, wu_ref, wd_ref, o_ref):
    i = pl.program_id(0)

    @pl.when(i < nv_ref[0])
    def _():
        x = x_ref[...]
        h = jnp.dot(x, wg_ref[0], preferred_element_type=F32)
        u = jnp.dot(x, wu_ref[0], preferred_element_type=F32)
        hid = (h * jax.nn.sigmoid(h)) * u * gate_ref[...]
        o_ref[...] = jnp.dot(hid.astype(BF16), wd_ref[0], preferred_element_type=F32)

    @pl.when(i >= nv_ref[0])
    def _():
        o_ref[...] = jnp.zeros_like(o_ref)


def _moe_grouped(tile_expert, n_valid, xs, gates, wg, wu, wd, *, tm):
    p, d = xs.shape
    f = wg.shape[2]
    return pl.pallas_call(
        _moe_kernel,
        grid_spec=pltpu.PrefetchScalarGridSpec(
            num_scalar_prefetch=2,
            grid=(p // tm,),
            in_specs=[
                pl.BlockSpec((tm, d), lambda i, te, nv: (i, 0)),
                pl.BlockSpec((tm, 1), lambda i, te, nv: (i, 0)),
                pl.BlockSpec((1, d, f), lambda i, te, nv: (te[i], 0, 0)),
                pl.BlockSpec((1, d, f), lambda i, te, nv: (te[i], 0, 0)),
                pl.BlockSpec((1, f, d), lambda i, te, nv: (te[i], 0, 0)),
            ],
            out_specs=pl.BlockSpec((tm, d), lambda i, te, nv: (i, 0)),
        ),
        out_shape=jax.ShapeDtypeStruct((p, d), F32),
        compiler_params=_params("arbitrary"),
        name="moe_grouped_mlp",
    )(tile_expert, n_valid, xs, gates, wg, wu, wd)


def _moe(xn, e1, e2, g1, g2, wg, wu, wd, *, tm):
    n, d = xn.shape
    n_exp = wg.shape[0]
    eids = jnp.concatenate([e1[:, 0], e2[:, 0]])
    gates = jnp.concatenate([g1[:, 0], g2[:, 0]])
    tokens = jnp.concatenate([jnp.arange(n, dtype=jnp.int32)] * 2)
    onehot = (eids[:, None] == jnp.arange(n_exp, dtype=jnp.int32)[None, :]).astype(jnp.int32)
    csum = jnp.cumsum(onehot, axis=0)
    rank = jnp.take_along_axis(csum, eids[:, None], axis=1)[:, 0] - 1
    counts = csum[-1]
    padded = ((counts + tm - 1) // tm) * tm
    ends = jnp.cumsum(padded)
    starts = ends - padded
    pos = starts[eids] + rank
    p_max = 2 * n + n_exp * tm
    n_tiles = p_max // tm
    src = jnp.zeros((p_max,), jnp.int32).at[pos].set(tokens)
    gate_sorted = jnp.zeros((p_max,), F32).at[pos].set(gates)
    n_valid = (ends[-1] // tm).astype(jnp.int32)
    tile_start = jnp.arange(n_tiles, dtype=jnp.int32) * tm
    tile_expert = jnp.searchsorted(ends, tile_start, side="right").astype(jnp.int32)
    last_expert = jnp.searchsorted(ends, ends[-1] - 1, side="right").astype(jnp.int32)
    tile_expert = jnp.minimum(tile_expert, last_expert)
    xs = jnp.take(xn, src, axis=0)
    ys = _moe_grouped(tile_expert, n_valid[None], xs, gate_sorted[:, None], wg, wu, wd, tm=tm)
    return jnp.take(ys, pos[:n], axis=0) + jnp.take(ys, pos[n:], axis=0)


def _final_norm_kernel(x_ref, g_ref, o_ref):
    o_ref[...] = _rms(x_ref[...], g_ref[...])


def _final_norm(x, g, *, tm):
    n, d = x.shape
    return pl.pallas_call(
        _final_norm_kernel,
        grid=(n // tm,),
        in_specs=[pl.BlockSpec((tm, d), lambda i: (i, 0)), pl.BlockSpec((1, d), lambda i: (0, 0))],
        out_specs=pl.BlockSpec((tm, d), lambda i: (i, 0)),
        out_shape=jax.ShapeDtypeStruct((n, d), F32),
        compiler_params=_params("arbitrary"),
        name="final_norm",
    )(x, g)


def kernel(x_prompt, x_sample, cache_attn_k, cache_attn_v, cache_conv, norm_mix, norm_ffn, norm_final,
           w_qkv, w_o, rel_table, w_pw1, b_pw1, w_dw, b_dw, ln_g, ln_b, w_pw2, b_pw2,
           w_group, b_group, w_router, b_router, w_gate, w_up, w_down):
    bp, tp, d = x_prompt.shape
    bs, ts, _ = x_sample.shape
    depth = norm_mix.shape[0]
    kv_keep = cache_attn_k.shape[2]
    conv_state = cache_conv.shape[2]
    assert kv_keep == BAND_PAST and tp % ROW_TILE == 0 and ts == CHUNK and ROW_TILE == BAND_PAST
    tm_s = min(ROW_TILE, bs * ts)

    groups = ((x_prompt, bp, tp), (x_sample, bs, ts))
    xs_cur = [x_prompt.reshape(bp * tp, d), x_sample.reshape(bs * ts, d)]
    kv_out = [[[], []], [[], []]]
    conv_out = [[], []]

    for layer in range(depth):
        gm = norm_mix[layer][None, :]
        gf = norm_ffn[layer][None, :]
        wr = jnp.concatenate([w_group[layer], w_router[layer]], axis=1)
        br = jnp.concatenate([b_group[layer], b_router[layer]])[None, :]
        wg = w_gate[layer].astype(BF16)
        wu = w_up[layer].astype(BF16)
        wd = w_down[layer].astype(BF16)
        ffn_in = []
        if layer % 2 == 0:
            a = layer // 2
            wqkv = w_qkv[a].astype(BF16)
            wo = w_o[a].astype(BF16)
            for gi, (_, b, t) in enumerate(groups):
                x3 = xs_cur[gi].reshape(b, t, d)
                if gi == 0:
                    q, kpad, vpad, k32, v32 = _qkv(x3, gm, wqkv, tm=ROW_TILE, pad_blocks=1)
                    bias = _band_bias(rel_table[a], ATTN_Q_ROWS)
                    att = _attention(q, kpad, vpad, bias, tb=ROW_TILE, tq=ATTN_Q_ROWS, invalid_rows=BAND_PAST)
                    kv_out[gi][0].append(k32.reshape(b, kv_keep, N_HEADS, d // N_HEADS))
                    kv_out[gi][1].append(v32.reshape(b, kv_keep, N_HEADS, d // N_HEADS))
                    tm = ROW_TILE
                else:
                    q, kn, vn, k32, v32 = _qkv(x3, gm, wqkv, tm=t, pad_blocks=0)
                    ck = cache_attn_k[a].reshape(b, kv_keep, d)
                    cv = cache_attn_v[a].reshape(b, kv_keep, d)
                    kpad = jnp.concatenate([ck.astype(BF16), kn], axis=1)
                    vpad = jnp.concatenate([cv.astype(BF16), vn], axis=1)
                    bias = _band_bias(rel_table[a], t)
                    att = _attention(q, kpad, vpad, bias, tb=t, tq=t, invalid_rows=0)
                    k_all = jnp.concatenate([ck, k32], axis=1)[:, -kv_keep:]
                    v_all = jnp.concatenate([cv, v32], axis=1)[:, -kv_keep:]
                    kv_out[gi][0].append(k_all.reshape(b, kv_keep, N_HEADS, d // N_HEADS))
                    kv_out[gi][1].append(v_all.reshape(b, kv_keep, N_HEADS, d // N_HEADS))
                    tm = tm_s
                ffn_in.append(_attn_out(att.reshape(b * t, d), xs_cur[gi], wo, gf, wr, br, tm=tm))
        else:
            c = layer // 2
            w1 = w_pw1[c].astype(BF16)
            w2 = w_pw2[c].astype(BF16)
            for gi, (_, b, t) in enumerate(groups):
                x3 = xs_cur[gi].reshape(b, t, d)
                state = jnp.zeros((b, conv_state, d), F32) if gi == 0 else cache_conv[c]
                outs = _conv_layer(x3, state, gm, w1, b_pw1[c][None, :], w_dw[c], b_dw[c][None, :],
                                   ln_g[c][None, :], ln_b[c][None, :], w2, b_pw2[c][None, :], gf, wr, br,
                                   tt=ROW_TILE if gi == 0 else t)
                ffn_in.append(outs[:6])
                conv_out[gi].append(outs[6])
        for gi in range(2):
            x1, xn, e1, e2, g1, g2 = ffn_in[gi]
            xs_cur[gi] = x1 + _moe(xn, e1, e2, g1, g2, wg, wu, wd, tm=MOE_ROW_TILE)

    gfin = norm_final[None, :]
    y_prompt = _final_norm(xs_cur[0], gfin, tm=ROW_TILE).reshape(bp, tp, d)
    y_sample = _final_norm(xs_cur[1], gfin, tm=tm_s).reshape(bs, ts, d)
    return (y_prompt, y_sample,
            jnp.stack(kv_out[0][0]), jnp.stack(kv_out[0][1]), jnp.stack(conv_out[0]),
            jnp.stack(kv_out[1][0]), jnp.stack(kv_out[1][1]), jnp.stack(conv_out[1]))
```

```python
import functools

import jax
import jax.numpy as jnp
from jax import lax
from jax.experimental import pallas as pl
from jax.experimental.pallas import tpu as pltpu

F32 = jnp.float32
BF16 = jnp.bfloat16
I32 = jnp.int32

CHUNK = 64
LEFT_CHUNKS = 8
BAND_PAST = LEFT_CHUNKS * CHUNK
N_HEADS = 16
REL_CLIP = 128
N_GROUPS = 4
EXPERTS_PER_GROUP = 8
N_EXPERTS = N_GROUPS * EXPERTS_PER_GROUP
N_ROUTER = N_GROUPS + N_EXPERTS
RMS_EPS = 1e-6
LN_EPS = 1e-5
LOG2_E = 1.4426950408889634

LANES = 128
SUBLANES = 8
VMEM_LIMIT_BYTES = 56 * 1024 * 1024

MASK_VALUE = -1e30
ROW_TILE = 512
ATTN_Q_ROWS = 128
MOE_ROW_TILE = 512
DMA_UNROLL = 8
CONV_HEAD = 32
CONV_ROW_BLOCK = 128


def _params(*sem):
    return pltpu.CompilerParams(dimension_semantics=sem, vmem_limit_bytes=VMEM_LIMIT_BYTES)


def _rms(x, g):
    return x * lax.rsqrt(jnp.mean(x * x, axis=-1, keepdims=True) + RMS_EPS) * g


def _store_rows(ref, val):
    rows, d = val.shape
    nb = d // LANES
    for c in range(nb):
        ref[pl.ds(c, rows, stride=nb), :] = val[:, c * LANES:(c + 1) * LANES]


def _load_rows(ref, rows, nb):
    return jnp.concatenate([ref[pl.ds(c, rows, stride=nb), :] for c in range(nb)], axis=-1)


def _qkv_kernel(x_ref, g_ref, w_ref, q_ref, k_ref, v_ref, k32_ref, v32_ref, *, pad_blocks, d):
    j = pl.program_id(1)

    @pl.when(j < pad_blocks)
    def _():
        k_ref[...] = jnp.zeros_like(k_ref)
        v_ref[...] = jnp.zeros_like(v_ref)

    @pl.when(j >= pad_blocks)
    def _():
        xn = _rms(x_ref[0], g_ref[...]).astype(BF16)
        q = jnp.dot(xn, w_ref[:, 0:d], preferred_element_type=F32)
        q_ref[0] = (q * (LOG2_E * float(d // N_HEADS) ** -0.5)).astype(BF16)
        k = jnp.dot(xn, w_ref[:, d:2 * d], preferred_element_type=F32)
        k_ref[0] = k.astype(BF16)
        k32_ref[0] = k
        v = jnp.dot(xn, w_ref[:, 2 * d:3 * d], preferred_element_type=F32)
        v_ref[0] = v.astype(BF16)
        v32_ref[0] = v


def _qkv(x, g, w, *, tm, pad_blocks):
    b, t, d = x.shape
    nt = t // tm
    xmap = lambda bi, j: (bi, jnp.maximum(j - pad_blocks, 0), 0)
    return pl.pallas_call(
        functools.partial(_qkv_kernel, pad_blocks=pad_blocks, d=d),
        grid=(b, nt + pad_blocks),
        in_specs=[
            pl.BlockSpec((1, tm, d), xmap),
            pl.BlockSpec((1, d), lambda bi, j: (0, 0)),
            pl.BlockSpec((d, 3 * d), lambda bi, j: (0, 0)),
        ],
        out_specs=[
            pl.BlockSpec((1, tm, d), xmap),
            pl.BlockSpec((1, tm, d), lambda bi, j: (bi, j, 0)),
            pl.BlockSpec((1, tm, d), lambda bi, j: (bi, j, 0)),
            pl.BlockSpec((1, tm, d), lambda bi, j: (bi, 0, 0)),
            pl.BlockSpec((1, tm, d), lambda bi, j: (bi, 0, 0)),
        ],
        out_shape=[
            jax.ShapeDtypeStruct((b, t, d), BF16),
            jax.ShapeDtypeStruct((b, t + pad_blocks * tm, d), BF16),
            jax.ShapeDtypeStruct((b, t + pad_blocks * tm, d), BF16),
            jax.ShapeDtypeStruct((b, tm, d), F32),
            jax.ShapeDtypeStruct((b, tm, d), F32),
        ],
        compiler_params=_params("arbitrary", "arbitrary"),
        name="qkv_proj",
    )(x, g, w)


def _attn_kernel(q_ref, kp_ref, kc_ref, vp_ref, vc_ref, bias_ref, o_ref, kbuf, vbuf,
                 *, tq, invalid_rows, n_heads):
    i = pl.program_id(1)
    tb = q_ref.shape[1]
    pad = kp_ref.shape[1]
    nk = tq + pad
    kbuf[0:pad, :] = kp_ref[0]
    kbuf[pad:pad + tb, :] = kc_ref[0]
    vbuf[0:pad, :] = vp_ref[0]
    vbuf[pad:pad + tb, :] = vc_ref[0]

    col = lax.broadcasted_iota(I32, (2 * tq, nk), 1)
    lane = lax.broadcasted_iota(I32, (1, LANES), 1)
    keep_lo = jnp.where(lane < LANES // 2, 1.0, 0.0).astype(BF16)
    keep_hi = jnp.where(lane < LANES // 2, 0.0, 1.0).astype(BF16)
    out_lo = lax.broadcasted_iota(I32, (tq, LANES), 1) < LANES // 2

    def sub_tile(s, carry, *, mask_positions):
        r0 = pl.multiple_of(s * tq, tq)
        pos_ok = col >= invalid_rows - i * tb - r0
        for hp in range(n_heads // 2):
            cols = slice(hp * LANES, (hp + 1) * LANES)
            q2 = q_ref[0, pl.ds(r0, tq), cols]
            k2 = kbuf[pl.ds(r0, nk), cols]
            v2 = vbuf[pl.ds(r0, nk), cols]
            qm = jnp.concatenate([q2 * keep_lo, q2 * keep_hi], axis=0)
            sc = lax.dot_general(qm, k2, (((1,), (1,)), ((), ())), preferred_element_type=F32)
            sc = sc + bias_ref[hp]
            if mask_positions:
                sc = jnp.where(pos_ok, sc, MASK_VALUE)
            m = jnp.max(sc, axis=-1, keepdims=True)
            p = jnp.exp2(sc - m)
            l = jnp.sum(p, axis=-1, keepdims=True)
            o = jnp.dot(p.astype(BF16), v2, preferred_element_type=F32) / l
            o_ref[0, pl.ds(r0, tq), cols] = jnp.where(out_lo, o[0:tq], o[tq:2 * tq]).astype(o_ref.dtype)
        return carry

    def run(mask_positions):
        lax.fori_loop(0, tb // tq, functools.partial(sub_tile, mask_positions=mask_positions), 0)

    if invalid_rows == 0:
        run(False)
    else:
        touches_invalid = i * tb < invalid_rows
        pl.when(touches_invalid)(lambda: run(True))
        pl.when(jnp.logical_not(touches_invalid))(lambda: run(False))


def _attention(q, kpad, vpad, bias, *, tb, tq, invalid_rows):
    b, t, d = q.shape
    pad = BAND_PAST
    cur0 = pad // tb
    return pl.pallas_call(
        functools.partial(_attn_kernel, tq=tq, invalid_rows=invalid_rows, n_heads=N_HEADS),
        grid=(b, t // tb),
        in_specs=[
            pl.BlockSpec((1, tb, d), lambda bi, i: (bi, i, 0)),
            pl.BlockSpec((1, pad, d), lambda bi, i: (bi, (i * tb) // pad, 0)),
            pl.BlockSpec((1, tb, d), lambda bi, i: (bi, cur0 + i, 0)),
            pl.BlockSpec((1, pad, d), lambda bi, i: (bi, (i * tb) // pad, 0)),
            pl.BlockSpec((1, tb, d), lambda bi, i: (bi, cur0 + i, 0)),
            pl.BlockSpec(bias.shape, lambda bi, i: (0, 0, 0)),
        ],
        out_specs=pl.BlockSpec((1, tb, d), lambda bi, i: (bi, i, 0)),
        out_shape=jax.ShapeDtypeStruct((b, t, d), BF16),
        scratch_shapes=[pltpu.VMEM((pad + tb, d), BF16), pltpu.VMEM((pad + tb, d), BF16)],
        compiler_params=_params("arbitrary", "arbitrary"),
        name="band_attention",
    )(q, kpad, kpad, vpad, vpad, bias)


def _band_bias(rel_table, tq):
    r = jnp.arange(tq)[:, None]
    c = jnp.arange(tq + BAND_PAST)[None, :]
    rel = jnp.clip(r - (c - BAND_PAST), -REL_CLIP, REL_CLIP) + REL_CLIP
    qc, kc = r // CHUNK, c // CHUNK
    in_band = (kc >= qc) & (kc <= qc + LEFT_CHUNKS)
    bias = jnp.where(in_band[None], LOG2_E * rel_table.astype(F32)[:, rel], MASK_VALUE)
    return bias.reshape(N_HEADS // 2, 2 * tq, tq + BAND_PAST)


def _router_logits(xn, wcat_ref, br_ref):
    xh = xn.astype(BF16)
    xl = (xn - xh.astype(F32)).astype(BF16)
    both = jnp.dot(xh, wcat_ref[...], preferred_element_type=F32)
    low = jnp.dot(xl, wcat_ref[:, 0:N_ROUTER], preferred_element_type=F32)
    return both[:, 0:N_ROUTER] + both[:, N_ROUTER:2 * N_ROUTER] + low + br_ref[...]


def _ffn_prologue(x1, gf_ref, wcat_ref, br_ref, tri_ref, first_step,
                  x1_ref, xn_ref, e1_ref, e2_ref, g1_ref, g2_ref, r1_ref, r2_ref, cnt_ref, carry):
    xn = _rms(x1, gf_ref[...])
    x1_ref[...] = x1
    _store_rows(xn_ref, xn)

    logits = _router_logits(xn, wcat_ref, br_ref)
    lane = lax.broadcasted_iota(I32, logits.shape, 1).astype(F32)
    is_group = lane < N_GROUPS
    neg_inf = -jnp.inf
    far = float(N_ROUTER)
    gl = jnp.where(is_group, logits, neg_inf)
    gmax = jnp.max(gl, axis=-1, keepdims=True)
    g_w = 1.0 / jnp.sum(jnp.exp(gl - gmax), axis=-1, keepdims=True)
    g_idx = jnp.min(jnp.where(gl == gmax, lane, far), axis=-1, keepdims=True)
    eidx = lane - N_GROUPS
    egroup = jnp.floor(eidx * (1.0 / EXPERTS_PER_GROUP))
    el = jnp.where(egroup == g_idx, logits, neg_inf)
    m1 = jnp.max(el, axis=-1, keepdims=True)
    i1 = jnp.min(jnp.where(el == m1, eidx, far), axis=-1, keepdims=True)
    el2 = jnp.where(eidx == i1, neg_inf, el)
    m2 = jnp.max(el2, axis=-1, keepdims=True)
    i2 = jnp.min(jnp.where(el2 == m2, eidx, far), axis=-1, keepdims=True)
    t = jnp.exp(m2 - m1)
    w_a = 1.0 / (1.0 + t)
    e1_ref[...] = i1.astype(I32)
    e2_ref[...] = i2.astype(I32)
    g1_ref[...] = g_w * w_a
    g2_ref[...] = g_w * (t * w_a)

    @pl.when(first_step)
    def _():
        carry[...] = jnp.zeros_like(carry)

    sel1 = eidx == i1
    sel2 = eidx == i2
    onehot = jnp.where(sel1, 1.0, jnp.where(sel2, 1.0, 0.0))
    before = jnp.dot(tri_ref[...], onehot.astype(BF16), preferred_element_type=F32) + carry[...]
    r1_ref[...] = jnp.sum(jnp.where(sel1, before, 0.0), axis=-1, keepdims=True).astype(I32)
    r2_ref[...] = jnp.sum(jnp.where(sel2, before, 0.0), axis=-1, keepdims=True).astype(I32)
    carry[...] = carry[...] + jnp.sum(onehot, axis=0, keepdims=True)
    cnt_ref[...] = carry[...]


def _ffn_out(n, d, tm, row):
    nb = d // LANES
    col = pl.BlockSpec((tm, 1), lambda *g: (row(*g), 0))
    specs = [pl.BlockSpec((tm, d), lambda *g: (row(*g), 0)),
             pl.BlockSpec((tm * nb, LANES), lambda *g: (row(*g), 0))] + [col] * 6 \
        + [pl.BlockSpec((1, N_ROUTER), lambda *g: (0, 0))]
    shapes = [jax.ShapeDtypeStruct((n, d), F32), jax.ShapeDtypeStruct((n * nb, LANES), F32),
              jax.ShapeDtypeStruct((n, 1), I32), jax.ShapeDtypeStruct((n, 1), I32),
              jax.ShapeDtypeStruct((n, 1), F32), jax.ShapeDtypeStruct((n, 1), F32),
              jax.ShapeDtypeStruct((n, 1), I32), jax.ShapeDtypeStruct((n, 1), I32),
              jax.ShapeDtypeStruct((1, N_ROUTER), F32)]
    return specs, shapes


def _strict_lower(tm):
    r = jnp.arange(tm)
    return (r[:, None] > r[None, :]).astype(BF16)


def _attn_out_kernel(a_ref, x_ref, wo_ref, gf_ref, wcat_ref, br_ref, tri_ref, *rest):
    outs, carry = rest[:-1], rest[-1]
    x1 = x_ref[...] + jnp.dot(a_ref[...], wo_ref[...], preferred_element_type=F32)
    _ffn_prologue(x1, gf_ref, wcat_ref, br_ref, tri_ref, pl.program_id(0) == 0, *outs, carry)


def _attn_out(a, x, wo, gf, wcat, br, *, tm):
    n, d = x.shape
    row = lambda i: (i, 0)
    fixed = lambda i: (0, 0)
    tri = _strict_lower(tm)
    specs, shapes = _ffn_out(n, d, tm, lambda i: i)
    return pl.pallas_call(
        _attn_out_kernel,
        grid=(n // tm,),
        in_specs=[
            pl.BlockSpec((tm, d), row), pl.BlockSpec((tm, d), row),
            pl.BlockSpec(wo.shape, fixed), pl.BlockSpec(gf.shape, fixed),
            pl.BlockSpec(wcat.shape, fixed), pl.BlockSpec(br.shape, fixed), pl.BlockSpec(tri.shape, fixed),
        ],
        out_specs=specs,
        out_shape=shapes,
        scratch_shapes=[pltpu.VMEM((1, N_ROUTER), F32)],
        compiler_params=_params("arbitrary"),
        name="attn_out_router",
    )(a, x, wo, gf, wcat, br, tri)


def _conv_kernel(x_ref, st_ref, gm_ref, w1_ref, b1_ref, wdw_ref, bdw_ref, lg_ref, lb_ref, w2_ref, b2_ref,
                 gf_ref, wcat_ref, br_ref, tri_ref, *rest, width):
    outs, ns_ref = rest[:9], rest[9]
    ubuf, rbuf, dwbuf, carry = rest[10:]
    bi = pl.program_id(0)
    t = pl.program_id(1)
    tt = x_ref.shape[1]
    d = x_ref.shape[2]
    nb = d // LANES
    state = width - 1
    lo = CONV_HEAD - state
    shifted_rows = rbuf.shape[1]
    rb = min(CONV_ROW_BLOCK, tt)

    @pl.when(t == 0)
    def _():
        for c in range(nb):
            ubuf[c, lo:CONV_HEAD, :] = st_ref[0, :, c * LANES:(c + 1) * LANES]

    x = x_ref[0]
    xn = _rms(x, gm_ref[...]).astype(BF16)
    a = jnp.dot(xn, w1_ref[:, 0:d], preferred_element_type=F32) + b1_ref[:, 0:d]
    g = jnp.dot(xn, w1_ref[:, d:2 * d], preferred_element_type=F32) + b1_ref[:, d:2 * d]
    u = a * jax.nn.sigmoid(g)
    for c in range(nb):
        ubuf[c, CONV_HEAD:CONV_HEAD + tt, :] = u[:, c * LANES:(c + 1) * LANES]
    ns_ref[0] = u[tt - state:tt, :]

    def lane_block(c, unused):
        for b in range(1, SUBLANES):
            rbuf[b - 1] = ubuf[c, b:b + shifted_rows, :]
        for r in range(tt // rb):
            r0 = r * rb
            acc = jnp.broadcast_to(bdw_ref[c], (rb, LANES))
            for j in range(width):
                a8, b = divmod(lo + j, SUBLANES)
                if b == 0:
                    src = ubuf[c, lo + j + r0:lo + j + r0 + rb, :]
                else:
                    src = rbuf[b - 1, a8 * SUBLANES + r0:a8 * SUBLANES + r0 + rb, :]
                acc = acc + src * wdw_ref[c, j:j + 1, :]
            dwbuf[c, r0:r0 + rb, :] = acc
        ubuf[c, lo:CONV_HEAD, :] = ubuf[c, lo + tt:CONV_HEAD + tt, :]
        return unused

    lax.fori_loop(0, nb, lane_block, 0)

    dw = jnp.concatenate([dwbuf[c] for c in range(nb)], axis=-1)
    mu = jnp.mean(dw, axis=-1, keepdims=True)
    xc = dw - mu
    z = xc * lax.rsqrt(jnp.mean(xc * xc, axis=-1, keepdims=True) + LN_EPS) * lg_ref[...] + lb_ref[...]
    z = z * jax.nn.sigmoid(z)
    x1 = x + jnp.dot(z.astype(BF16), w2_ref[...], preferred_element_type=F32) + b2_ref[...]
    first = jnp.logical_and(bi == 0, t == 0)
    _ffn_prologue(x1, gf_ref, wcat_ref, br_ref, tri_ref, first, *outs, carry)


def _conv_layer(x, state, gm, w1, b1, wdw, bdw, lg, lb, w2, b2, gf, wcat, br, *, tt):
    b, t, d = x.shape
    width = wdw.shape[0]
    nb = d // LANES
    nt = t // tt
    wdw3 = wdw.reshape(width, nb, LANES).transpose(1, 0, 2)
    bdw3 = bdw.reshape(nb, 1, LANES)
    tri = _strict_lower(tt)
    fixed2 = lambda bi, ti: (0, 0)
    fixed3 = lambda bi, ti: (0, 0, 0)
    specs, shapes = _ffn_out(b * t, d, tt, lambda bi, ti: bi * nt + ti)
    shifted_rows = tt + ((CONV_HEAD - 1) // SUBLANES) * SUBLANES
    return pl.pallas_call(
        functools.partial(_conv_kernel, width=width),
        grid=(b, nt),
        in_specs=[
            pl.BlockSpec((1, tt, d), lambda bi, ti: (bi, ti, 0)),
            pl.BlockSpec((1, width - 1, d), lambda bi, ti: (bi, 0, 0)),
            pl.BlockSpec(gm.shape, fixed2), pl.BlockSpec(w1.shape, fixed2), pl.BlockSpec(b1.shape, fixed2),
            pl.BlockSpec(wdw3.shape, fixed3), pl.BlockSpec(bdw3.shape, fixed3),
            pl.BlockSpec(lg.shape, fixed2), pl.BlockSpec(lb.shape, fixed2),
            pl.BlockSpec(w2.shape, fixed2), pl.BlockSpec(b2.shape, fixed2),
            pl.BlockSpec(gf.shape, fixed2), pl.BlockSpec(wcat.shape, fixed2), pl.BlockSpec(br.shape, fixed2),
            pl.BlockSpec(tri.shape, fixed2),
        ],
        out_specs=specs + [pl.BlockSpec((1, width - 1, d), lambda bi, ti: (bi, 0, 0))],
        out_shape=shapes + [jax.ShapeDtypeStruct((b, width - 1, d), F32)],
        scratch_shapes=[
            pltpu.VMEM((nb, CONV_HEAD + tt, LANES), F32),
            pltpu.VMEM((SUBLANES - 1, shifted_rows, LANES), F32),
            pltpu.VMEM((nb, tt, LANES), F32),
            pltpu.VMEM((1, N_ROUTER), F32),
        ],
        compiler_params=_params("arbitrary", "arbitrary"),
        name="conv_module_router",
    )(x, state, gm, w1, b1, wdw3, bdw3, lg, lb, w2, b2, gf, wcat, br, tri)


def _slot(starts_ref, meta_ref, tm, which, r):
    e = meta_ref[0, 0, which * tm + r]
    return starts_ref[e] + meta_ref[0, 0, (2 + which) * tm + r]


def _row_copy(src, src_row, dst, dst_row, sem, nb):
    return pltpu.make_async_copy(src.at[pl.ds(pl.multiple_of(src_row * nb, nb), nb), :],
                                 dst.at[pl.ds(pl.multiple_of(dst_row * nb, nb), nb), :], sem)


def _dispatch_kernel(starts_ref, ends_ref, meta_ref, xn_ref, xs_hbm, zbuf, sem, zsem, *, tm, moe_tm, nb):
    i = pl.program_id(0)

    @pl.when(i == 0)
    def _():
        zbuf[...] = jnp.zeros_like(zbuf)

        def tile_copy(e):
            return pltpu.make_async_copy(
                zbuf, xs_hbm.at[pl.ds(pl.multiple_of((ends_ref[e] - moe_tm) * nb, nb), moe_tm * nb), :], zsem)

        for e in range(N_EXPERTS):
            @pl.when(ends_ref[e] > starts_ref[e])
            def _():
                tile_copy(e).start()
        for e in range(N_EXPERTS):
            @pl.when(ends_ref[e] > starts_ref[e])
            def _():
                tile_copy(e).wait()

        def unused_tile(j):
            return pltpu.make_async_copy(
                zbuf, xs_hbm.at[pl.ds(pl.multiple_of(j * (moe_tm * nb), moe_tm * nb), moe_tm * nb), :], zsem)

        first_unused = ends_ref[N_EXPERTS - 1] // moe_tm
        n_tiles = xs_hbm.shape[0] // (moe_tm * nb)
        lax.fori_loop(first_unused, n_tiles, lambda j, c: (unused_tile(j).start(), c)[1], 0)
        lax.fori_loop(first_unused, n_tiles, lambda j, c: (unused_tile(j).wait(), c)[1], 0)

    def copies(r):
        return [_row_copy(xn_ref, r, xs_hbm, _slot(starts_ref, meta_ref, tm, w, r), sem, nb) for w in range(2)]

    def issue(g, carry):
        for k in range(DMA_UNROLL):
            for cp in copies(g * DMA_UNROLL + k):
                cp.start()
        return carry

    def drain(g, carry):
        for k in range(DMA_UNROLL):
            for cp in copies(g * DMA_UNROLL + k):
                cp.wait()
        return carry

    lax.fori_loop(0, tm // DMA_UNROLL, issue, 0)
    lax.fori_loop(0, tm // DMA_UNROLL, drain, 0)


def _dispatch(starts, ends, meta, xn_rows, *, n, d, tm, moe_tm, p_max):
    nb = d // LANES
    return pl.pallas_call(
        functools.partial(_dispatch_kernel, tm=tm, moe_tm=moe_tm, nb=nb),
        grid_spec=pltpu.PrefetchScalarGridSpec(
            num_scalar_prefetch=2,
            grid=(n // tm,),
            in_specs=[
                pl.BlockSpec((1, 1, 4 * tm), lambda i, s, e: (i, 0, 0), memory_space=pltpu.SMEM),
                pl.BlockSpec((tm * nb, LANES), lambda i, s, e: (i, 0)),
            ],
            out_specs=pl.BlockSpec(memory_space=pl.ANY),
            scratch_shapes=[pltpu.VMEM((moe_tm * nb, LANES), F32),
                            pltpu.SemaphoreType.DMA, pltpu.SemaphoreType.DMA],
        ),
        out_shape=jax.ShapeDtypeStruct((p_max * nb, LANES), F32),
        compiler_params=_params("arbitrary"),
        name="moe_dispatch",
    )(starts, ends, meta, xn_rows)


def _moe_kernel(te_ref, nv_ref, x_ref, wg_ref, wu_ref, wd_ref, o_ref, *, tm, nb):
    i = pl.program_id(0)

    @pl.when(i < nv_ref[0])
    def _():
        x = _load_rows(x_ref, tm, nb).astype(BF16)
        h = jnp.dot(x, wg_ref[0], preferred_element_type=F32)
        u = jnp.dot(x, wu_ref[0], preferred_element_type=F32)
        hid = (h * jax.nn.sigmoid(h)) * u
        _store_rows(o_ref, jnp.dot(hid.astype(BF16), wd_ref[0], preferred_element_type=F32))

    @pl.when(i >= nv_ref[0])
    def _():
        o_ref[...] = jnp.zeros_like(o_ref)


def _moe_grouped(tile_expert, n_valid, xs_rows, wg, wu, wd, *, tm, d):
    nb = d // LANES
    p = xs_rows.shape[0] // nb
    f = wg.shape[2]
    tile = lambda i, te, nv: (jnp.minimum(i, nv[0] - 1), 0)
    wmap = lambda i, te, nv: (te[i], 0, 0)
    return pl.pallas_call(
        functools.partial(_moe_kernel, tm=tm, nb=nb),
        grid_spec=pltpu.PrefetchScalarGridSpec(
            num_scalar_prefetch=2,
            grid=(p // tm,),
            in_specs=[
                pl.BlockSpec((tm * nb, LANES), tile),
                pl.BlockSpec((1, d, f), wmap), pl.BlockSpec((1, d, f), wmap), pl.BlockSpec((1, f, d), wmap),
            ],
            out_specs=pl.BlockSpec((tm * nb, LANES), lambda i, te, nv: (i, 0)),
        ),
        out_shape=jax.ShapeDtypeStruct((p * nb, LANES), F32),
        compiler_params=_params("arbitrary"),
        name="moe_grouped_mlp",
    )(tile_expert, n_valid, xs_rows, wg, wu, wd)


def _combine_kernel(starts_ref, meta_ref, x1_ref, g1_ref, g2_ref, gfin_ref, ys_hbm, o_ref, buf1, buf2, sem,
                    *, tm, nb, final_norm):
    bufs = (buf1, buf2)

    def copies(r):
        return [_row_copy(ys_hbm, _slot(starts_ref, meta_ref, tm, w, r), bufs[w], r, sem, nb) for w in range(2)]

    def issue(g, carry):
        for k in range(DMA_UNROLL):
            for cp in copies(g * DMA_UNROLL + k):
                cp.start()
        return carry

    def drain(g, carry):
        for k in range(DMA_UNROLL):
            for cp in copies(g * DMA_UNROLL + k):
                cp.wait()
        return carry

    lax.fori_loop(0, tm // DMA_UNROLL, issue, 0)
    lax.fori_loop(0, tm // DMA_UNROLL, drain, 0)
    y = x1_ref[...] + g1_ref[...] * _load_rows(buf1, tm, nb) + g2_ref[...] * _load_rows(buf2, tm, nb)
    o_ref[...] = _rms(y, gfin_ref[...]) if final_norm else y


def _combine(starts, meta, x1, g1, g2, gfin, ys_rows, *, tm, final_norm):
    n, d = x1.shape
    nb = d // LANES
    row = lambda i, s: (i, 0)
    return pl.pallas_call(
        functools.partial(_combine_kernel, tm=tm, nb=nb, final_norm=final_norm),
        grid_spec=pltpu.PrefetchScalarGridSpec(
            num_scalar_prefetch=1,
            grid=(n // tm,),
            in_specs=[
                pl.BlockSpec((1, 1, 4 * tm), lambda i, s: (i, 0, 0), memory_space=pltpu.SMEM),
                pl.BlockSpec((tm, d), row), pl.BlockSpec((tm, 1), row), pl.BlockSpec((tm, 1), row),
                pl.BlockSpec((1, d), lambda i, s: (0, 0)),
                pl.BlockSpec(memory_space=pl.ANY),
            ],
            out_specs=pl.BlockSpec((tm, d), row),
            scratch_shapes=[pltpu.VMEM((tm * nb, LANES), F32), pltpu.VMEM((tm * nb, LANES), F32),
                            pltpu.SemaphoreType.DMA],
        ),
        out_shape=jax.ShapeDtypeStruct((n, d), F32),
        compiler_params=_params("arbitrary"),
        name="moe_combine",
    )(starts, meta, x1, g1, g2, gfin, ys_rows)


def _moe(ffn, wg, wu, wd, gfin, *, tm, final_norm):
    x1, xn_rows, e1, e2, g1, g2, r1, r2, cnt = ffn
    n, d = x1.shape
    moe_tm = MOE_ROW_TILE
    counts = cnt[0, N_GROUPS:].astype(I32)
    padded = ((counts + moe_tm - 1) // moe_tm) * moe_tm
    ends = jnp.cumsum(padded)
    starts = ends - padded
    p_max = 2 * n + N_EXPERTS * moe_tm
    n_tiles = p_max // moe_tm
    n_valid = ends[-1] // moe_tm
    tile_start = jnp.arange(n_tiles, dtype=I32) * moe_tm
    tile_expert = jnp.sum((tile_start[:, None] >= ends[None, :]).astype(I32), axis=1)
    last_expert = jnp.sum((ends[-1] - 1 >= ends).astype(I32))
    tile_expert = jnp.minimum(tile_expert, last_expert)
    meta = jnp.concatenate([v.reshape(n // tm, tm) for v in (e1, e2, r1, r2)], axis=1).reshape(n // tm, 1, 4 * tm)
    xs_rows = _dispatch(starts, ends, meta, xn_rows, n=n, d=d, tm=tm, moe_tm=moe_tm, p_max=p_max)
    ys_rows = _moe_grouped(tile_expert, n_valid[None], xs_rows, wg, wu, wd, tm=moe_tm, d=d)
    return _combine(starts, meta, x1, g1, g2, gfin, ys_rows, tm=tm, final_norm=final_norm)


def kernel(x_prompt, x_sample, cache_attn_k, cache_attn_v, cache_conv, norm_mix, norm_ffn, norm_final,
           w_qkv, w_o, rel_table, w_pw1, b_pw1, w_dw, b_dw, ln_g, ln_b, w_pw2, b_pw2,
           w_group, b_group, w_router, b_router, w_gate, w_up, w_down):
    bp, tp, d = x_prompt.shape
    bs, ts, _ = x_sample.shape
    depth = norm_mix.shape[0]
    kv_keep = cache_attn_k.shape[2]
    conv_state = cache_conv.shape[2]
    assert kv_keep == BAND_PAST and tp % ROW_TILE == 0 and ts == CHUNK and ROW_TILE == BAND_PAST
    assert w_group.shape[2] == N_GROUPS and w_router.shape[2] == N_EXPERTS
    tm_s = min(ROW_TILE, bs * ts)

    groups = ((bp, tp, ROW_TILE), (bs, ts, tm_s))
    xs_cur = [x_prompt.reshape(bp * tp, d), x_sample.reshape(bs * ts, d)]
    kv_out = [[[], []], [[], []]]
    conv_out = [[], []]
    gfin = norm_final[None, :]

    for layer in range(depth):
        gm = norm_mix[layer][None, :]
        gf = norm_ffn[layer][None, :]
        wr = jnp.concatenate([w_group[layer], w_router[layer]], axis=1)
        wr_hi = wr.astype(BF16)
        wcat = jnp.concatenate([wr_hi, (wr - wr_hi.astype(F32)).astype(BF16)], axis=1)
        br = jnp.concatenate([b_group[layer], b_router[layer]])[None, :]
        wg = w_gate[layer].astype(BF16)
        wu = w_up[layer].astype(BF16)
        wd = w_down[layer].astype(BF16)
        ffn = []
        if layer % 2 == 0:
            a = layer // 2
            wqkv = w_qkv[a].astype(BF16)
            wo = w_o[a].astype(BF16)
            for gi, (b, t, tm) in enumerate(groups):
                x3 = xs_cur[gi].reshape(b, t, d)
                if gi == 0:
                    q, kpad, vpad, k32, v32 = _qkv(x3, gm, wqkv, tm=ROW_TILE, pad_blocks=1)
                    bias = _band_bias(rel_table[a], ATTN_Q_ROWS)
                    att = _attention(q, kpad, vpad, bias, tb=ROW_TILE, tq=ATTN_Q_ROWS, invalid_rows=BAND_PAST)
                    k_new, v_new = k32, v32
                else:
                    q, kn, vn, k32, v32 = _qkv(x3, gm, wqkv, tm=t, pad_blocks=0)
                    ck = cache_attn_k[a].reshape(b, kv_keep, d)
                    cv = cache_attn_v[a].reshape(b, kv_keep, d)
                    kpad = jnp.concatenate([ck.astype(BF16), kn], axis=1)
                    vpad = jnp.concatenate([cv.astype(BF16), vn], axis=1)
                    bias = _band_bias(rel_table[a], t)
                    att = _attention(q, kpad, vpad, bias, tb=t, tq=t, invalid_rows=0)
                    k_new = jnp.concatenate([ck, k32], axis=1)[:, -kv_keep:]
                    v_new = jnp.concatenate([cv, v32], axis=1)[:, -kv_keep:]
                kv_out[gi][0].append(k_new.reshape(b, kv_keep, N_HEADS, d // N_HEADS))
                kv_out[gi][1].append(v_new.reshape(b, kv_keep, N_HEADS, d // N_HEADS))
                ffn.append(_attn_out(att.reshape(b * t, d), xs_cur[gi], wo, gf, wcat, br, tm=tm))
        else:
            c = layer // 2
            w1 = w_pw1[c].astype(BF16)
            w2 = w_pw2[c].astype(BF16)
            for gi, (b, t, tm) in enumerate(groups):
                x3 = xs_cur[gi].reshape(b, t, d)
                state = jnp.zeros((b, conv_state, d), F32) if gi == 0 else cache_conv[c]
                outs = _conv_layer(x3, state, gm, w1, b_pw1[c][None, :], w_dw[c], b_dw[c], ln_g[c][None, :],
                                   ln_b[c][None, :], w2, b_pw2[c][None, :], gf, wcat, br,
                                   tt=ROW_TILE if gi == 0 else t)
                ffn.append(outs[:9])
                conv_out[gi].append(outs[9])
        last = layer == depth - 1
        for gi, (b, t, tm) in enumerate(groups):
            xs_cur[gi] = _moe(ffn[gi], wg, wu, wd, gfin, tm=tm, final_norm=last)

    y_prompt = xs_cur[0].reshape(bp, tp, d)
    y_sample = xs_cur[1].reshape(bs, ts, d)
    return (y_prompt, y_sample,
            jnp.stack(kv_out[0][0]), jnp.stack(kv_out[0][1]), jnp.stack(conv_out[0]),
            jnp.stack(kv_out[1][0]), jnp.stack(kv_out[1][1]), jnp.stack(conv_out[1]))
```

```python
import functools

import numpy as np
import jax
import jax.numpy as jnp
from jax import lax
from jax.experimental import pallas as pl
from jax.experimental.pallas import tpu as pltpu

F32 = jnp.float32
BF16 = jnp.bfloat16
I32 = jnp.int32

CHUNK = 64
LEFT_CHUNKS = 8
BAND_PAST = LEFT_CHUNKS * CHUNK
N_HEADS = 16
REL_CLIP = 128
N_GROUPS = 4
EXPERTS_PER_GROUP = 8
N_EXPERTS = N_GROUPS * EXPERTS_PER_GROUP
N_ROUTER = N_GROUPS + N_EXPERTS
RMS_EPS = 1e-6
LN_EPS = 1e-5
LOG2_E = 1.4426950408889634

LANES = 128
SUBLANES = 8
VMEM_LIMIT_BYTES = 56 * 1024 * 1024

MASK_VALUE = -1e30
ROW_TILE = 512
ATTN_Q_ROWS = 128
MOE_ROW_TILE = 512
DMA_UNROLL = 8
CONV_HEAD = 32
CONV_ROW_BLOCK = 128


def _params(*sem):
    return pltpu.CompilerParams(dimension_semantics=sem, vmem_limit_bytes=VMEM_LIMIT_BYTES)


def _rms(x, g):
    return x * lax.rsqrt(jnp.mean(x * x, axis=-1, keepdims=True) + RMS_EPS) * g


def _store_rows(ref, val):
    rows, d = val.shape
    nb = d // LANES
    for c in range(nb):
        ref[pl.ds(c, rows, stride=nb), :] = val[:, c * LANES:(c + 1) * LANES]


def _load_rows(ref, rows, nb):
    return jnp.concatenate([ref[pl.ds(c, rows, stride=nb), :] for c in range(nb)], axis=-1)


def _qkv_kernel(x_ref, g_ref, w_ref, q_ref, k_ref, v_ref, k32_ref, v32_ref, *, pad_blocks, d):
    j = pl.program_id(1)

    @pl.when(j < pad_blocks)
    def _():
        k_ref[...] = jnp.zeros_like(k_ref)
        v_ref[...] = jnp.zeros_like(v_ref)

    @pl.when(j >= pad_blocks)
    def _():
        xn = _rms(x_ref[0], g_ref[...]).astype(BF16)
        q = jnp.dot(xn, w_ref[:, 0:d], preferred_element_type=F32) * (LOG2_E * float(d // N_HEADS) ** -0.5)
        k = jnp.dot(xn, w_ref[:, d:2 * d], preferred_element_type=F32)
        v = jnp.dot(xn, w_ref[:, 2 * d:3 * d], preferred_element_type=F32)
        k32_ref[0] = k
        v32_ref[0] = v
        for hp in range(d // LANES):
            cols = slice(hp * LANES, (hp + 1) * LANES)
            q_ref[0, hp] = q[:, cols].astype(BF16)
            k_ref[0, hp] = k[:, cols].astype(BF16)
            v_ref[0, hp] = v[:, cols].astype(BF16)


def _qkv(x, g, w, *, tm, pad_blocks):
    b, t, d = x.shape
    nt = t // tm
    npair = d // LANES
    xmap = lambda bi, j: (bi, jnp.maximum(j - pad_blocks, 0), 0)
    return pl.pallas_call(
        functools.partial(_qkv_kernel, pad_blocks=pad_blocks, d=d),
        grid=(b, nt + pad_blocks),
        in_specs=[
            pl.BlockSpec((1, tm, d), xmap),
            pl.BlockSpec((1, d), lambda bi, j: (0, 0)),
            pl.BlockSpec((d, 3 * d), lambda bi, j: (0, 0)),
        ],
        out_specs=[
            pl.BlockSpec((1, npair, tm, LANES), lambda bi, j: (bi, 0, jnp.maximum(j - pad_blocks, 0), 0)),
            pl.BlockSpec((1, npair, tm, LANES), lambda bi, j: (bi, 0, j, 0)),
            pl.BlockSpec((1, npair, tm, LANES), lambda bi, j: (bi, 0, j, 0)),
            pl.BlockSpec((1, tm, d), lambda bi, j: (bi, 0, 0)),
            pl.BlockSpec((1, tm, d), lambda bi, j: (bi, 0, 0)),
        ],
        out_shape=[
            jax.ShapeDtypeStruct((b, npair, t, LANES), BF16),
            jax.ShapeDtypeStruct((b, npair, t + pad_blocks * tm, LANES), BF16),
            jax.ShapeDtypeStruct((b, npair, t + pad_blocks * tm, LANES), BF16),
            jax.ShapeDtypeStruct((b, tm, d), F32),
            jax.ShapeDtypeStruct((b, tm, d), F32),
        ],
        compiler_params=_params("arbitrary", "arbitrary"),
        name="qkv_proj",
    )(x, g, w)


def _attn_kernel(q_ref, kp_ref, kc_ref, vp_ref, vc_ref, bias_ref, o_ref,
                 kbuf, vbuf, s_even, s_odd, p_even, p_odd, l_even, l_odd, *, tq, invalid_rows):
    i = pl.program_id(1)
    npair, tb = q_ref.shape[1], q_ref.shape[2]
    pad = kp_ref.shape[2]
    nk = tq + pad
    n_items = (tb // tq) * npair
    assert n_items % 2 == 0 and n_items >= 4 and npair & (npair - 1) == 0
    pair_bits = npair.bit_length() - 1
    kbuf[:, 0:pad, :] = kp_ref[0]
    kbuf[:, pad:pad + tb, :] = kc_ref[0]
    vbuf[:, 0:pad, :] = vp_ref[0]
    vbuf[:, pad:pad + tb, :] = vc_ref[0]

    col = lax.broadcasted_iota(I32, (2 * tq, nk), 1)
    lane = lax.broadcasted_iota(I32, (1, LANES), 1)
    keep_lo = jnp.where(lane < LANES // 2, 1.0, 0.0).astype(BF16)
    keep_hi = jnp.where(lane < LANES // 2, 0.0, 1.0).astype(BF16)
    out_lo = lax.broadcasted_iota(I32, (tq, LANES), 1) < LANES // 2
    s_bufs, p_bufs, l_bufs = (s_even, s_odd), (p_even, p_odd), (l_even, l_odd)

    def where(item):
        hp = jnp.bitwise_and(item, npair - 1)
        r0 = pl.multiple_of(jnp.right_shift(item, pair_bits) * tq, tq)
        return hp, r0

    def scores(item, slot):
        hp, r0 = where(item)
        q2 = q_ref[0, hp, pl.ds(r0, tq), :]
        qm = jnp.concatenate([q2 * keep_lo, q2 * keep_hi], axis=0)
        s_bufs[slot][...] = lax.dot_general(qm, kbuf[hp, pl.ds(r0, nk), :], (((1,), (1,)), ((), ())),
                                            preferred_element_type=F32)

    def softmax(item, slot, mask_positions):
        hp, r0 = where(item)
        sc = s_bufs[slot][...] + bias_ref[hp]
        if mask_positions:
            sc = jnp.where(col >= invalid_rows - i * tb - r0, sc, MASK_VALUE)
        p = jnp.exp2(sc - jnp.max(sc, axis=-1, keepdims=True))
        l_bufs[slot][...] = jnp.sum(p, axis=-1, keepdims=True)
        p_bufs[slot][...] = p.astype(BF16)

    def weighted_values(item, slot):
        hp, r0 = where(item)
        o = jnp.dot(p_bufs[slot][...], vbuf[hp, pl.ds(r0, nk), :], preferred_element_type=F32) / l_bufs[slot][...]
        o_ref[0, hp, pl.ds(r0, tq), :] = jnp.where(out_lo, o[0:tq], o[tq:2 * tq]).astype(o_ref.dtype)

    def run(mask_positions):
        scores(0, 0)
        scores(1, 1)
        softmax(0, 0, mask_positions)

        def two_items(g, carry):
            t = 2 * g
            scores(t, 0)
            softmax(t - 1, 1, mask_positions)
            weighted_values(t - 2, 0)
            scores(t + 1, 1)
            softmax(t, 0, mask_positions)
            weighted_values(t - 1, 1)
            return carry

        lax.fori_loop(1, n_items // 2, two_items, 0)
        softmax(n_items - 1, 1, mask_positions)
        weighted_values(n_items - 2, 0)
        weighted_values(n_items - 1, 1)

    if invalid_rows == 0:
        run(False)
    else:
        touches_invalid = i * tb < invalid_rows
        pl.when(touches_invalid)(lambda: run(True))
        pl.when(jnp.logical_not(touches_invalid))(lambda: run(False))


def _attention(q, kpad, vpad, bias, *, tb, tq, invalid_rows):
    b, npair, t, _ = q.shape
    pad = BAND_PAST
    cur0 = pad // tb
    nk = tq + pad
    past = lambda bi, i: (bi, 0, (i * tb) // pad, 0)
    cur = lambda bi, i: (bi, 0, cur0 + i, 0)
    return pl.pallas_call(
        functools.partial(_attn_kernel, tq=tq, invalid_rows=invalid_rows),
        grid=(b, t // tb),
        in_specs=[
            pl.BlockSpec((1, npair, tb, LANES), lambda bi, i: (bi, 0, i, 0)),
            pl.BlockSpec((1, npair, pad, LANES), past), pl.BlockSpec((1, npair, tb, LANES), cur),
            pl.BlockSpec((1, npair, pad, LANES), past), pl.BlockSpec((1, npair, tb, LANES), cur),
            pl.BlockSpec(bias.shape, lambda bi, i: (0, 0, 0)),
        ],
        out_specs=pl.BlockSpec((1, npair, tb, LANES), lambda bi, i: (bi, 0, i, 0)),
        out_shape=jax.ShapeDtypeStruct(q.shape, BF16),
        scratch_shapes=[pltpu.VMEM((npair, pad + tb, LANES), BF16), pltpu.VMEM((npair, pad + tb, LANES), BF16),
                        pltpu.VMEM((2 * tq, nk), F32), pltpu.VMEM((2 * tq, nk), F32),
                        pltpu.VMEM((2 * tq, nk), BF16), pltpu.VMEM((2 * tq, nk), BF16),
                        pltpu.VMEM((2 * tq, 1), F32), pltpu.VMEM((2 * tq, 1), F32)],
        compiler_params=_params("arbitrary", "arbitrary"),
        name="band_attention",
    )(q, kpad, kpad, vpad, vpad, bias)


def _band_bias(rel_table, tq):
    nk = tq + BAND_PAST
    dist = np.arange(-(tq - 1), nk)
    per_dist = LOG2_E * rel_table.astype(F32)[:, np.clip(BAND_PAST - dist, -REL_CLIP, REL_CLIP) + REL_CLIP]
    bias = jnp.stack([per_dist[:, tq - 1 - r:tq - 1 - r + nk] for r in range(tq)], axis=1)
    qc, kc = np.arange(tq)[:, None] // CHUNK, np.arange(nk)[None, :] // CHUNK
    in_band = (kc >= qc) & (kc <= qc + LEFT_CHUNKS)
    bias = jnp.where(in_band[None], bias, MASK_VALUE)
    return bias.reshape(N_HEADS // 2, 2 * tq, nk)


def _router_logits(xn, wcat_ref, br_ref):
    xh = xn.astype(BF16)
    xl = (xn - xh.astype(F32)).astype(BF16)
    both = jnp.dot(xh, wcat_ref[...], preferred_element_type=F32)
    low = jnp.dot(xl, wcat_ref[:, 0:N_ROUTER], preferred_element_type=F32)
    return both[:, 0:N_ROUTER] + both[:, N_ROUTER:2 * N_ROUTER] + low + br_ref[...]


def _ffn_prologue(x1, gf_ref, wcat_ref, br_ref, tri_ref, first_step,
                  x1_ref, xn_ref, e1_ref, e2_ref, g1_ref, g2_ref, r1_ref, r2_ref, cnt_ref, carry):
    xn = _rms(x1, gf_ref[...])
    x1_ref[...] = x1
    _store_rows(xn_ref, xn)

    logits = _router_logits(xn, wcat_ref, br_ref)
    lane = lax.broadcasted_iota(I32, logits.shape, 1).astype(F32)
    is_group = lane < N_GROUPS
    neg_inf = -jnp.inf
    far = float(N_ROUTER)
    gl = jnp.where(is_group, logits, neg_inf)
    gmax = jnp.max(gl, axis=-1, keepdims=True)
    g_w = 1.0 / jnp.sum(jnp.exp(gl - gmax), axis=-1, keepdims=True)
    g_idx = jnp.min(jnp.where(gl == gmax, lane, far), axis=-1, keepdims=True)
    eidx = lane - N_GROUPS
    egroup = jnp.floor(eidx * (1.0 / EXPERTS_PER_GROUP))
    el = jnp.where(egroup == g_idx, logits, neg_inf)
    m1 = jnp.max(el, axis=-1, keepdims=True)
    i1 = jnp.min(jnp.where(el == m1, eidx, far), axis=-1, keepdims=True)
    el2 = jnp.where(eidx == i1, neg_inf, el)
    m2 = jnp.max(el2, axis=-1, keepdims=True)
    i2 = jnp.min(jnp.where(el2 == m2, eidx, far), axis=-1, keepdims=True)
    t = jnp.exp(m2 - m1)
    w_a = 1.0 / (1.0 + t)
    e1_ref[...] = i1.astype(I32)
    e2_ref[...] = i2.astype(I32)
    g1_ref[...] = g_w * w_a
    g2_ref[...] = g_w * (t * w_a)

    @pl.when(first_step)
    def _():
        carry[...] = jnp.zeros_like(carry)

    sel1 = eidx == i1
    sel2 = eidx == i2
    onehot = jnp.where(sel1, 1.0, jnp.where(sel2, 1.0, 0.0))
    before = jnp.dot(tri_ref[...], onehot.astype(BF16), preferred_element_type=F32) + carry[...]
    r1_ref[...] = jnp.sum(jnp.where(sel1, before, 0.0), axis=-1, keepdims=True).astype(I32)
    r2_ref[...] = jnp.sum(jnp.where(sel2, before, 0.0), axis=-1, keepdims=True).astype(I32)
    carry[...] = carry[...] + jnp.sum(onehot, axis=0, keepdims=True)
    cnt_ref[...] = carry[...]


def _ffn_out(n, d, tm, row):
    nb = d // LANES
    col = pl.BlockSpec((tm, 1), lambda *g: (row(*g), 0))
    specs = [pl.BlockSpec((tm, d), lambda *g: (row(*g), 0)),
             pl.BlockSpec((tm * nb, LANES), lambda *g: (row(*g), 0))] + [col] * 6 \
        + [pl.BlockSpec((1, N_ROUTER), lambda *g: (0, 0))]
    shapes = [jax.ShapeDtypeStruct((n, d), F32), jax.ShapeDtypeStruct((n * nb, LANES), F32),
              jax.ShapeDtypeStruct((n, 1), I32), jax.ShapeDtypeStruct((n, 1), I32),
              jax.ShapeDtypeStruct((n, 1), F32), jax.ShapeDtypeStruct((n, 1), F32),
              jax.ShapeDtypeStruct((n, 1), I32), jax.ShapeDtypeStruct((n, 1), I32),
              jax.ShapeDtypeStruct((1, N_ROUTER), F32)]
    return specs, shapes


def _strict_lower(tm):
    r = jnp.arange(tm)
    return (r[:, None] > r[None, :]).astype(BF16)


def _attn_out_kernel(a_ref, x_ref, wo_ref, gf_ref, wcat_ref, br_ref, tri_ref, *rest):
    outs, carry = rest[:-1], rest[-1]
    nbatch, npair, rows, _ = a_ref.shape
    a = jnp.concatenate([a_ref[:, hp].reshape(nbatch * rows, LANES) for hp in range(npair)], axis=-1)
    x1 = x_ref[...] + jnp.dot(a, wo_ref[...], preferred_element_type=F32)
    _ffn_prologue(x1, gf_ref, wcat_ref, br_ref, tri_ref, pl.program_id(0) == 0, *outs, carry)


def _attn_out(a, x, wo, gf, wcat, br, *, tm):
    n, d = x.shape
    b, npair, t, _ = a.shape
    row = lambda i: (i, 0)
    fixed = lambda i: (0, 0)
    tri = _strict_lower(tm)
    specs, shapes = _ffn_out(n, d, tm, lambda i: i)
    if t >= tm:
        a_spec = pl.BlockSpec((1, npair, tm, LANES), lambda i: (i // (t // tm), 0, i % (t // tm), 0))
    else:
        a_spec = pl.BlockSpec((tm // t, npair, t, LANES), lambda i: (i, 0, 0, 0))
    return pl.pallas_call(
        _attn_out_kernel,
        grid=(n // tm,),
        in_specs=[
            a_spec, pl.BlockSpec((tm, d), row),
            pl.BlockSpec(wo.shape, fixed), pl.BlockSpec(gf.shape, fixed),
            pl.BlockSpec(wcat.shape, fixed), pl.BlockSpec(br.shape, fixed), pl.BlockSpec(tri.shape, fixed),
        ],
        out_specs=specs,
        out_shape=shapes,
        scratch_shapes=[pltpu.VMEM((1, N_ROUTER), F32)],
        compiler_params=_params("arbitrary"),
        name="attn_out_router",
    )(a, x, wo, gf, wcat, br, tri)


def _conv_kernel(x_ref, st_ref, gm_ref, w1_ref, b1_ref, wdw_ref, bdw_ref, lg_ref, lb_ref, w2_ref, b2_ref,
                 gf_ref, wcat_ref, br_ref, tri_ref, *rest, width):
    outs, ns_ref = rest[:9], rest[9]
    ubuf, rbuf, dwbuf, carry = rest[10:]
    bi = pl.program_id(0)
    t = pl.program_id(1)
    tt = x_ref.shape[1]
    d = x_ref.shape[2]
    nb = d // LANES
    state = width - 1
    lo = CONV_HEAD - state
    shifted_rows = rbuf.shape[1]
    rb = min(CONV_ROW_BLOCK, tt)

    @pl.when(t == 0)
    def _():
        for c in range(nb):
            ubuf[c, lo:CONV_HEAD, :] = st_ref[0, :, c * LANES:(c + 1) * LANES]

    x = x_ref[0]
    xn = _rms(x, gm_ref[...]).astype(BF16)

    def glu(p):
        hid = jnp.dot(xn, w1_ref[p], preferred_element_type=F32) + b1_ref[p]
        for k in range(2):
            u = hid[:, k * LANES:(k + 1) * LANES] * jax.nn.sigmoid(hid[:, (2 + k) * LANES:(3 + k) * LANES])
            ubuf[2 * p + k, CONV_HEAD:CONV_HEAD + tt, :] = u
            ns_ref[0, 2 * p + k] = u[tt - state:tt, :]

    def depthwise(c):
        for b in range(1, SUBLANES):
            rbuf[b - 1] = ubuf[c, b:b + shifted_rows, :]
        for r in range(tt // rb):
            r0 = r * rb
            acc = jnp.broadcast_to(bdw_ref[c], (rb, LANES))
            for j in range(width):
                a8, b = divmod(lo + j, SUBLANES)
                if b == 0:
                    src = ubuf[c, lo + j + r0:lo + j + r0 + rb, :]
                else:
                    src = rbuf[b - 1, a8 * SUBLANES + r0:a8 * SUBLANES + r0 + rb, :]
                acc = acc + src * wdw_ref[c, j:j + 1, :]
            dwbuf[c, r0:r0 + rb, :] = acc
        ubuf[c, lo:CONV_HEAD, :] = ubuf[c, lo + tt:CONV_HEAD + tt, :]

    glu(0)
    for p in range(nb // 2):
        if p + 1 < nb // 2:
            glu(p + 1)
        depthwise(2 * p)
        depthwise(2 * p + 1)

    dw = jnp.concatenate([dwbuf[c] for c in range(nb)], axis=-1)
    mu = jnp.mean(dw, axis=-1, keepdims=True)
    xc = dw - mu
    z = xc * lax.rsqrt(jnp.mean(xc * xc, axis=-1, keepdims=True) + LN_EPS) * lg_ref[...] + lb_ref[...]
    z = z * jax.nn.sigmoid(z)
    x1 = x + jnp.dot(z.astype(BF16), w2_ref[...], preferred_element_type=F32) + b2_ref[...]
    first = jnp.logical_and(bi == 0, t == 0)
    _ffn_prologue(x1, gf_ref, wcat_ref, br_ref, tri_ref, first, *outs, carry)


def _conv_layer(x, state, gm, w1, b1, wdw, bdw, lg, lb, w2, b2, gf, wcat, br, *, tt):
    b, t, d = x.shape
    width = wdw.shape[0]
    nb = d // LANES
    nt = t // tt
    wdw3 = wdw.reshape(width, nb, LANES).transpose(1, 0, 2)
    bdw3 = bdw.reshape(nb, 1, LANES)
    pair_cols = lambda m: jnp.concatenate([m[..., :d].reshape(-1, nb // 2, 2 * LANES),
                                           m[..., d:].reshape(-1, nb // 2, 2 * LANES)], axis=2).transpose(1, 0, 2)
    w1 = pair_cols(w1)
    b1 = pair_cols(b1)
    tri = _strict_lower(tt)
    fixed2 = lambda bi, ti: (0, 0)
    fixed3 = lambda bi, ti: (0, 0, 0)
    specs, shapes = _ffn_out(b * t, d, tt, lambda bi, ti: bi * nt + ti)
    shifted_rows = tt + ((CONV_HEAD - 1) // SUBLANES) * SUBLANES
    return pl.pallas_call(
        functools.partial(_conv_kernel, width=width),
        grid=(b, nt),
        in_specs=[
            pl.BlockSpec((1, tt, d), lambda bi, ti: (bi, ti, 0)),
            pl.BlockSpec((1, width - 1, d), lambda bi, ti: (bi, 0, 0)),
            pl.BlockSpec(gm.shape, fixed2), pl.BlockSpec(w1.shape, fixed3), pl.BlockSpec(b1.shape, fixed3),
            pl.BlockSpec(wdw3.shape, fixed3), pl.BlockSpec(bdw3.shape, fixed3),
            pl.BlockSpec(lg.shape, fixed2), pl.BlockSpec(lb.shape, fixed2),
            pl.BlockSpec(w2.shape, fixed2), pl.BlockSpec(b2.shape, fixed2),
            pl.BlockSpec(gf.shape, fixed2), pl.BlockSpec(wcat.shape, fixed2), pl.BlockSpec(br.shape, fixed2),
            pl.BlockSpec(tri.shape, fixed2),
        ],
        out_specs=specs + [pl.BlockSpec((1, nb, width - 1, LANES), lambda bi, ti: (bi, 0, 0, 0))],
        out_shape=shapes + [jax.ShapeDtypeStruct((b, nb, width - 1, LANES), F32)],
        scratch_shapes=[
            pltpu.VMEM((nb, CONV_HEAD + tt, LANES), F32),
            pltpu.VMEM((SUBLANES - 1, shifted_rows, LANES), F32),
            pltpu.VMEM((nb, tt, LANES), F32),
            pltpu.VMEM((1, N_ROUTER), F32),
        ],
        compiler_params=_params("arbitrary", "arbitrary"),
        name="conv_module_router",
    )(x, state, gm, w1, b1, wdw3, bdw3, lg, lb, w2, b2, gf, wcat, br, tri)


def _slot(starts_ref, meta_ref, tm, which, r):
    e = meta_ref[0, 0, which * tm + r]
    return starts_ref[e] + meta_ref[0, 0, (2 + which) * tm + r]


def _row_copy(src, src_row, dst, dst_row, sem, nb):
    return pltpu.make_async_copy(src.at[pl.ds(pl.multiple_of(src_row * nb, nb), nb), :],
                                 dst.at[pl.ds(pl.multiple_of(dst_row * nb, nb), nb), :], sem)


def _dispatch_kernel(starts_ref, ends_ref, meta_ref, xn_ref, xs_hbm, zbuf, sem, zsem, *, tm, moe_tm, nb):
    i = pl.program_id(0)

    @pl.when(i == 0)
    def _():
        zbuf[...] = jnp.zeros_like(zbuf)

        def tile_copy(e):
            return pltpu.make_async_copy(
                zbuf, xs_hbm.at[pl.ds(pl.multiple_of((ends_ref[e] - moe_tm) * nb, nb), moe_tm * nb), :], zsem)

        for e in range(N_EXPERTS):
            @pl.when(ends_ref[e] > starts_ref[e])
            def _():
                tile_copy(e).start()
        for e in range(N_EXPERTS):
            @pl.when(ends_ref[e] > starts_ref[e])
            def _():
                tile_copy(e).wait()

        def unused_tile(j):
            return pltpu.make_async_copy(
                zbuf, xs_hbm.at[pl.ds(pl.multiple_of(j * (moe_tm * nb), moe_tm * nb), moe_tm * nb), :], zsem)

        first_unused = ends_ref[N_EXPERTS - 1] // moe_tm
        n_tiles = xs_hbm.shape[0] // (moe_tm * nb)
        lax.fori_loop(first_unused, n_tiles, lambda j, c: (unused_tile(j).start(), c)[1], 0)
        lax.fori_loop(first_unused, n_tiles, lambda j, c: (unused_tile(j).wait(), c)[1], 0)

    def copies(r):
        return [_row_copy(xn_ref, r, xs_hbm, _slot(starts_ref, meta_ref, tm, w, r), sem, nb) for w in range(2)]

    def issue(g, carry):
        for k in range(DMA_UNROLL):
            for queue, cp in enumerate(copies(g * DMA_UNROLL + k)):
                cp.start(priority=queue)
        return carry

    def drain(g, carry):
        for k in range(DMA_UNROLL):
            for cp in copies(g * DMA_UNROLL + k):
                cp.wait()
        return carry

    lax.fori_loop(0, tm // DMA_UNROLL, issue, 0)
    lax.fori_loop(0, tm // DMA_UNROLL, drain, 0)


def _dispatch(starts, ends, meta, xn_rows, *, n, d, tm, moe_tm, p_max):
    nb = d // LANES
    return pl.pallas_call(
        functools.partial(_dispatch_kernel, tm=tm, moe_tm=moe_tm, nb=nb),
        grid_spec=pltpu.PrefetchScalarGridSpec(
            num_scalar_prefetch=2,
            grid=(n // tm,),
            in_specs=[
                pl.BlockSpec((1, 1, 4 * tm), lambda i, s, e: (i, 0, 0), memory_space=pltpu.SMEM),
                pl.BlockSpec((tm * nb, LANES), lambda i, s, e: (i, 0)),
            ],
            out_specs=pl.BlockSpec(memory_space=pl.ANY),
            scratch_shapes=[pltpu.VMEM((moe_tm * nb, LANES), F32),
                            pltpu.SemaphoreType.DMA, pltpu.SemaphoreType.DMA],
        ),
        out_shape=jax.ShapeDtypeStruct((p_max * nb, LANES), F32),
        compiler_params=_params("arbitrary"),
        name="moe_dispatch",
    )(starts, ends, meta, xn_rows)


def _moe_kernel(te_ref, nv_ref, x_ref, wg_ref, wu_ref, wd_ref, o_ref, wgu, wdn, *, tm, nb):
    i = pl.program_id(0)
    f = wg_ref.shape[3]

    @pl.when(jnp.logical_or(i == 0, te_ref[i] != te_ref[jnp.maximum(i - 1, 0)]))
    def _():
        wgu[:, 0:f] = wg_ref[0, 0].astype(BF16)
        wgu[:, f:2 * f] = wu_ref[0, 0].astype(BF16)
        wdn[...] = wd_ref[0, 0].astype(BF16)

    @pl.when(i < nv_ref[0])
    def _():
        x = _load_rows(x_ref, tm, nb).astype(BF16)
        hu = jnp.dot(x, wgu[...], preferred_element_type=F32)
        h, u = hu[:, 0:f], hu[:, f:2 * f]
        hid = (h * jax.nn.sigmoid(h)) * u
        _store_rows(o_ref, jnp.dot(hid.astype(BF16), wdn[...], preferred_element_type=F32))

    @pl.when(i >= nv_ref[0])
    def _():
        o_ref[...] = jnp.zeros_like(o_ref)


def _moe_grouped(tile_expert, n_valid, xs_rows, wg, wu, wd, *, layer, tm, d):
    nb = d // LANES
    p = xs_rows.shape[0] // nb
    f = wg.shape[3]
    tile = lambda i, te, nv: (jnp.minimum(i, nv[0] - 1), 0)
    wmap = lambda i, te, nv: (layer, te[i], 0, 0)
    return pl.pallas_call(
        functools.partial(_moe_kernel, tm=tm, nb=nb),
        grid_spec=pltpu.PrefetchScalarGridSpec(
            num_scalar_prefetch=2,
            grid=(p // tm,),
            in_specs=[
                pl.BlockSpec((tm * nb, LANES), tile),
                pl.BlockSpec((1, 1, d, f), wmap), pl.BlockSpec((1, 1, d, f), wmap),
                pl.BlockSpec((1, 1, f, d), wmap),
            ],
            out_specs=pl.BlockSpec((tm * nb, LANES), lambda i, te, nv: (i, 0)),
            scratch_shapes=[pltpu.VMEM((d, 2 * f), BF16), pltpu.VMEM((f, d), BF16)],
        ),
        out_shape=jax.ShapeDtypeStruct((p * nb, LANES), F32),
        compiler_params=_params("arbitrary"),
        name="moe_grouped_mlp",
    )(tile_expert, n_valid, xs_rows, wg, wu, wd)


def _combine_kernel(starts_ref, meta_ref, x1_ref, g1_ref, g2_ref, gfin_ref, ys_hbm, o_ref, buf1, buf2, sem,
                    *, tm, nb, final_norm):
    bufs = (buf1, buf2)

    def copies(r):
        return [_row_copy(ys_hbm, _slot(starts_ref, meta_ref, tm, w, r), bufs[w], r, sem, nb) for w in range(2)]

    def issue(g, carry):
        for k in range(DMA_UNROLL):
            for queue, cp in enumerate(copies(g * DMA_UNROLL + k)):
                cp.start(priority=queue)
        return carry

    def drain(g, carry):
        for k in range(DMA_UNROLL):
            for cp in copies(g * DMA_UNROLL + k):
                cp.wait()
        return carry

    lax.fori_loop(0, tm // DMA_UNROLL, issue, 0)
    lax.fori_loop(0, tm // DMA_UNROLL, drain, 0)
    y = x1_ref[...] + g1_ref[...] * _load_rows(buf1, tm, nb) + g2_ref[...] * _load_rows(buf2, tm, nb)
    o_ref[...] = _rms(y, gfin_ref[...]) if final_norm else y


def _combine(starts, meta, x1, g1, g2, gfin, ys_rows, *, tm, final_norm):
    n, d = x1.shape
    nb = d // LANES
    row = lambda i, s: (i, 0)
    return pl.pallas_call(
        functools.partial(_combine_kernel, tm=tm, nb=nb, final_norm=final_norm),
        grid_spec=pltpu.PrefetchScalarGridSpec(
            num_scalar_prefetch=1,
            grid=(n // tm,),
            in_specs=[
                pl.BlockSpec((1, 1, 4 * tm), lambda i, s: (i, 0, 0), memory_space=pltpu.SMEM),
                pl.BlockSpec((tm, d), row), pl.BlockSpec((tm, 1), row), pl.BlockSpec((tm, 1), row),
                pl.BlockSpec((1, d), lambda i, s: (0, 0)),
                pl.BlockSpec(memory_space=pl.ANY),
            ],
            out_specs=pl.BlockSpec((tm, d), row),
            scratch_shapes=[pltpu.VMEM((tm * nb, LANES), F32), pltpu.VMEM((tm * nb, LANES), F32),
                            pltpu.SemaphoreType.DMA],
        ),
        out_shape=jax.ShapeDtypeStruct((n, d), F32),
        compiler_params=_params("arbitrary"),
        name="moe_combine",
    )(starts, meta, x1, g1, g2, gfin, ys_rows)


def _moe(ffn, wg, wu, wd, gfin, *, layer, tm, final_norm):
    x1, xn_rows, e1, e2, g1, g2, r1, r2, cnt = ffn
    n, d = x1.shape
    moe_tm = MOE_ROW_TILE
    counts = cnt[0, N_GROUPS:].astype(I32)
    padded = ((counts + moe_tm - 1) // moe_tm) * moe_tm
    ends = jnp.cumsum(padded)
    starts = ends - padded
    p_max = 2 * n + N_EXPERTS * moe_tm
    n_tiles = p_max // moe_tm
    n_valid = ends[-1] // moe_tm
    tile_start = jnp.arange(n_tiles, dtype=I32) * moe_tm
    tile_expert = jnp.sum((tile_start[:, None] >= ends[None, :]).astype(I32), axis=1)
    last_expert = jnp.sum((ends[-1] - 1 >= ends).astype(I32))
    tile_expert = jnp.minimum(tile_expert, last_expert)
    meta = jnp.concatenate([v.reshape(n // tm, tm) for v in (e1, e2, r1, r2)], axis=1).reshape(n // tm, 1, 4 * tm)
    xs_rows = _dispatch(starts, ends, meta, xn_rows, n=n, d=d, tm=tm, moe_tm=moe_tm, p_max=p_max)
    ys_rows = _moe_grouped(tile_expert, n_valid[None], xs_rows, wg, wu, wd, layer=layer, tm=moe_tm, d=d)
    return _combine(starts, meta, x1, g1, g2, gfin, ys_rows, tm=tm, final_norm=final_norm)


def kernel(x_prompt, x_sample, cache_attn_k, cache_attn_v, cache_conv, norm_mix, norm_ffn, norm_final,
           w_qkv, w_o, rel_table, w_pw1, b_pw1, w_dw, b_dw, ln_g, ln_b, w_pw2, b_pw2,
           w_group, b_group, w_router, b_router, w_gate, w_up, w_down):
    bp, tp, d = x_prompt.shape
    bs, ts, _ = x_sample.shape
    depth = norm_mix.shape[0]
    kv_keep = cache_attn_k.shape[2]
    conv_state = cache_conv.shape[2]
    assert kv_keep == BAND_PAST and tp % ROW_TILE == 0 and ts == CHUNK and ROW_TILE == BAND_PAST
    assert w_group.shape[2] == N_GROUPS and w_router.shape[2] == N_EXPERTS
    tm_s = min(ROW_TILE, bs * ts)

    groups = ((bp, tp, ROW_TILE), (bs, ts, tm_s))
    xs_cur = [x_prompt.reshape(bp * tp, d), x_sample.reshape(bs * ts, d)]
    kv_out = [[[], []], [[], []]]
    conv_out = [[], []]
    gfin = norm_final[None, :]

    for layer in range(depth):
        gm = norm_mix[layer][None, :]
        gf = norm_ffn[layer][None, :]
        wr = jnp.concatenate([w_group[layer], w_router[layer]], axis=1)
        wr_hi = wr.astype(BF16)
        wcat = jnp.concatenate([wr_hi, (wr - wr_hi.astype(F32)).astype(BF16)], axis=1)
        br = jnp.concatenate([b_group[layer], b_router[layer]])[None, :]
        ffn = []
        if layer % 2 == 0:
            a = layer // 2
            wqkv = w_qkv[a].astype(BF16)
            wo = w_o[a].astype(BF16)
            for gi, (b, t, tm) in enumerate(groups):
                x3 = xs_cur[gi].reshape(b, t, d)
                if gi == 0:
                    q, kpad, vpad, k32, v32 = _qkv(x3, gm, wqkv, tm=ROW_TILE, pad_blocks=1)
                    bias = _band_bias(rel_table[a], ATTN_Q_ROWS)
                    att = _attention(q, kpad, vpad, bias, tb=ROW_TILE, tq=ATTN_Q_ROWS, invalid_rows=BAND_PAST)
                    k_new, v_new = k32, v32
                else:
                    q, kn, vn, k32, v32 = _qkv(x3, gm, wqkv, tm=t, pad_blocks=0)
                    ck = cache_attn_k[a].reshape(b, kv_keep, d)
                    cv = cache_attn_v[a].reshape(b, kv_keep, d)
                    pair_major = lambda c: c.astype(BF16).reshape(b, kv_keep, d // LANES, LANES).transpose(0, 2, 1, 3)
                    kpad = jnp.concatenate([pair_major(ck), kn], axis=2)
                    vpad = jnp.concatenate([pair_major(cv), vn], axis=2)
                    bias = _band_bias(rel_table[a], t)
                    att = _attention(q, kpad, vpad, bias, tb=t, tq=t, invalid_rows=0)
                    k_new = jnp.concatenate([ck, k32], axis=1)[:, -kv_keep:]
                    v_new = jnp.concatenate([cv, v32], axis=1)[:, -kv_keep:]
                kv_out[gi][0].append(k_new.reshape(b, kv_keep, N_HEADS, d // N_HEADS))
                kv_out[gi][1].append(v_new.reshape(b, kv_keep, N_HEADS, d // N_HEADS))
                ffn.append(_attn_out(att, xs_cur[gi], wo, gf, wcat, br, tm=tm))
        else:
            c = layer // 2
            w1 = w_pw1[c].astype(BF16)
            w2 = w_pw2[c].astype(BF16)
            for gi, (b, t, tm) in enumerate(groups):
                x3 = xs_cur[gi].reshape(b, t, d)
                state = jnp.zeros((b, conv_state, d), F32) if gi == 0 else cache_conv[c]
                outs = _conv_layer(x3, state, gm, w1, b_pw1[c][None, :], w_dw[c], b_dw[c], ln_g[c][None, :],
                                   ln_b[c][None, :], w2, b_pw2[c][None, :], gf, wcat, br,
                                   tt=ROW_TILE if gi == 0 else t)
                ffn.append(outs[:9])
                conv_out[gi].append(outs[9].transpose(0, 2, 1, 3).reshape(b, conv_state, d))
        last = layer == depth - 1
        for gi, (b, t, tm) in enumerate(groups):
            xs_cur[gi] = _moe(ffn[gi], w_gate, w_up, w_down, gfin, layer=layer, tm=tm, final_norm=last)

    y_prompt = xs_cur[0].reshape(bp, tp, d)
    y_sample = xs_cur[1].reshape(bs, ts, d)
    return (y_prompt, y_sample,
            jnp.stack(kv_out[0][0]), jnp.stack(kv_out[0][1]), jnp.stack(conv_out[0]),
            jnp.stack(kv_out[1][0]), jnp.stack(kv_out[1][1]), jnp.stack(conv_out[1]))
```

```python
import functools
import math

import numpy as np
import jax
import jax.numpy as jnp
from jax import lax
from jax.experimental import pallas as pl
from jax.experimental.pallas import tpu as pltpu

F32 = jnp.float32
BF16 = jnp.bfloat16
I32 = jnp.int32

CHUNK = 64
LEFT_CHUNKS = 8
BAND_PAST = LEFT_CHUNKS * CHUNK
N_HEADS = 16
REL_CLIP = 128
N_GROUPS = 4
EXPERTS_PER_GROUP = 8
N_EXPERTS = N_GROUPS * EXPERTS_PER_GROUP
N_ROUTER = N_GROUPS + N_EXPERTS
RMS_EPS = 1e-6
LN_EPS = 1e-5
LOG2_E = 1.4426950408889634

LANES = 128
SUBLANES = 8
VMEM_LIMIT_BYTES = 56 * 1024 * 1024

MASK_VALUE = -1e30
ROW_TILE = 512
ATTN_Q_ROWS = 128
MOE_ROW_TILE = 512
MOE_IO_TILE = 1024
DMA_UNROLL = 8
CONV_HEAD = 32
CONV_ROW_BLOCK = 128


def _params(*sem):
    return pltpu.CompilerParams(dimension_semantics=sem, vmem_limit_bytes=VMEM_LIMIT_BYTES)


def _rms(x, g):
    return x * lax.rsqrt(jnp.mean(x * x, axis=-1, keepdims=True) + RMS_EPS) * g


def _store_rows(ref, val):
    rows, d = val.shape
    nb = d // LANES
    for c in range(nb):
        ref[pl.ds(c, rows, stride=nb), :] = val[:, c * LANES:(c + 1) * LANES]


def _load_rows(ref, rows, nb):
    return jnp.concatenate([ref[pl.ds(c, rows, stride=nb), :] for c in range(nb)], axis=-1)


def _pack_bf16_pairs(x):
    half = x.shape[1] // 2
    bits = pltpu.bitcast(x.astype(BF16).astype(F32), jnp.uint32)
    return jnp.right_shift(bits[:, :half], jnp.uint32(16)) | bits[:, half:]


def _unpack_bf16_pairs(w):
    lo = pltpu.bitcast(jnp.left_shift(w, jnp.uint32(16)), F32)
    hi = pltpu.bitcast(w & jnp.uint32(0xFFFF0000), F32)
    return jnp.concatenate([lo, hi], axis=-1).astype(BF16)


def _qkv_kernel(x_ref, g_ref, w_ref, q_ref, k_ref, v_ref, k32_ref, v32_ref, *, pad_blocks, d):
    j = pl.program_id(1)

    @pl.when(j < pad_blocks)
    def _():
        k_ref[...] = jnp.zeros_like(k_ref)
        v_ref[...] = jnp.zeros_like(v_ref)

    @pl.when(j >= pad_blocks)
    def _():
        xn = _rms(x_ref[0], g_ref[...]).astype(BF16)
        q = jnp.dot(xn, w_ref[:, 0:d], preferred_element_type=F32) * (LOG2_E * float(d // N_HEADS) ** -0.5)
        k = jnp.dot(xn, w_ref[:, d:2 * d], preferred_element_type=F32)
        v = jnp.dot(xn, w_ref[:, 2 * d:3 * d], preferred_element_type=F32)
        k32_ref[0] = k
        v32_ref[0] = v
        for hp in range(d // LANES):
            cols = slice(hp * LANES, (hp + 1) * LANES)
            q_ref[0, hp] = q[:, cols].astype(BF16)
            k_ref[0, hp] = k[:, cols].astype(BF16)
            v_ref[0, hp] = v[:, cols].astype(BF16)


def _qkv(x, g, w, *, tm, pad_blocks):
    b, t, d = x.shape
    nt = t // tm
    npair = d // LANES
    xmap = lambda bi, j: (bi, jnp.maximum(j - pad_blocks, 0), 0)
    return pl.pallas_call(
        functools.partial(_qkv_kernel, pad_blocks=pad_blocks, d=d),
        grid=(b, nt + pad_blocks),
        in_specs=[
            pl.BlockSpec((1, tm, d), xmap),
            pl.BlockSpec((1, d), lambda bi, j: (0, 0)),
            pl.BlockSpec((d, 3 * d), lambda bi, j: (0, 0)),
        ],
        out_specs=[
            pl.BlockSpec((1, npair, tm, LANES), lambda bi, j: (bi, 0, jnp.maximum(j - pad_blocks, 0), 0)),
            pl.BlockSpec((1, npair, tm, LANES), lambda bi, j: (bi, 0, j, 0)),
            pl.BlockSpec((1, npair, tm, LANES), lambda bi, j: (bi, 0, j, 0)),
            pl.BlockSpec((1, tm, d), lambda bi, j: (bi, 0, 0)),
            pl.BlockSpec((1, tm, d), lambda bi, j: (bi, 0, 0)),
        ],
        out_shape=[
            jax.ShapeDtypeStruct((b, npair, t, LANES), BF16),
            jax.ShapeDtypeStruct((b, npair, t + pad_blocks * tm, LANES), BF16),
            jax.ShapeDtypeStruct((b, npair, t + pad_blocks * tm, LANES), BF16),
            jax.ShapeDtypeStruct((b, tm, d), F32),
            jax.ShapeDtypeStruct((b, tm, d), F32),
        ],
        compiler_params=_params("arbitrary", "arbitrary"),
        name="qkv_proj",
    )(x, g, w)


def _attn_kernel(q_ref, kp_ref, kc_ref, vp_ref, vc_ref, bias_ref, o_ref,
                 kbuf, vbuf, s_even, s_odd, p_even, p_odd, l_even, l_odd, *, tq, invalid_rows):
    i = pl.program_id(1)
    npair, tb = q_ref.shape[1], q_ref.shape[2]
    pad = kp_ref.shape[2]
    nk = tq + pad
    n_items = (tb // tq) * npair
    assert n_items % 2 == 0 and n_items >= 4 and npair & (npair - 1) == 0
    pair_bits = npair.bit_length() - 1
    kbuf[:, 0:pad, :] = kp_ref[0]
    kbuf[:, pad:pad + tb, :] = kc_ref[0]
    vbuf[:, 0:pad, :] = vp_ref[0]
    vbuf[:, pad:pad + tb, :] = vc_ref[0]

    key = lax.broadcasted_iota(I32, (nk, 2 * tq), 0)
    lane = lax.broadcasted_iota(I32, (1, LANES), 1)
    keep_lo = jnp.where(lane < LANES // 2, 1.0, 0.0).astype(BF16)
    keep_hi = jnp.where(lane < LANES // 2, 0.0, 1.0).astype(BF16)
    out_lo = lax.broadcasted_iota(I32, (tq, LANES), 1) < LANES // 2
    s_bufs, p_bufs, l_bufs = (s_even, s_odd), (p_even, p_odd), (l_even, l_odd)

    def where(item):
        hp = jnp.bitwise_and(item, npair - 1)
        r0 = pl.multiple_of(jnp.right_shift(item, pair_bits) * tq, tq)
        return hp, r0

    def scores(item, slot):
        hp, r0 = where(item)
        q2 = q_ref[0, hp, pl.ds(r0, tq), :]
        qm = jnp.concatenate([q2 * keep_lo, q2 * keep_hi], axis=0)
        s_bufs[slot][...] = lax.dot_general(kbuf[hp, pl.ds(r0, nk), :], qm, (((1,), (1,)), ((), ())),
                                            preferred_element_type=F32)

    def softmax(item, slot, mask_positions):
        hp, r0 = where(item)
        sc = s_bufs[slot][...] + bias_ref[hp]
        if mask_positions:
            sc = jnp.where(key >= invalid_rows - i * tb - r0, sc, MASK_VALUE)
        p = jnp.exp2(sc - jnp.max(sc, axis=0, keepdims=True))
        l_bufs[slot][...] = jnp.sum(p, axis=0, keepdims=True)
        p_bufs[slot][...] = p.astype(BF16)

    def weighted_values(item, slot):
        hp, r0 = where(item)
        o_t = lax.dot_general(vbuf[hp, pl.ds(r0, nk), :], p_bufs[slot][...], (((0,), (0,)), ((), ())),
                              preferred_element_type=F32) / l_bufs[slot][...]
        o = o_t.T
        o_ref[0, hp, pl.ds(r0, tq), :] = jnp.where(out_lo, o[0:tq], o[tq:2 * tq]).astype(o_ref.dtype)

    def run(mask_positions):
        scores(0, 0)
        scores(1, 1)
        softmax(0, 0, mask_positions)

        def two_items(g, carry):
            t = 2 * g
            scores(t, 0)
            softmax(t - 1, 1, mask_positions)
            weighted_values(t - 2, 0)
            scores(t + 1, 1)
            softmax(t, 0, mask_positions)
            weighted_values(t - 1, 1)
            return carry

        lax.fori_loop(1, n_items // 2, two_items, 0)
        softmax(n_items - 1, 1, mask_positions)
        weighted_values(n_items - 2, 0)
        weighted_values(n_items - 1, 1)

    if invalid_rows == 0:
        run(False)
    else:
        touches_invalid = i * tb < invalid_rows
        pl.when(touches_invalid)(lambda: run(True))
        pl.when(jnp.logical_not(touches_invalid))(lambda: run(False))


def _attention(q, kpad, vpad, bias, *, tb, tq, invalid_rows):
    b, npair, t, _ = q.shape
    pad = BAND_PAST
    cur0 = pad // tb
    nk = tq + pad
    past = lambda bi, i: (bi, 0, (i * tb) // pad, 0)
    cur = lambda bi, i: (bi, 0, cur0 + i, 0)
    return pl.pallas_call(
        functools.partial(_attn_kernel, tq=tq, invalid_rows=invalid_rows),
        grid=(b, t // tb),
        in_specs=[
            pl.BlockSpec((1, npair, tb, LANES), lambda bi, i: (bi, 0, i, 0)),
            pl.BlockSpec((1, npair, pad, LANES), past), pl.BlockSpec((1, npair, tb, LANES), cur),
            pl.BlockSpec((1, npair, pad, LANES), past), pl.BlockSpec((1, npair, tb, LANES), cur),
            pl.BlockSpec(bias.shape, lambda bi, i: (0, 0, 0)),
        ],
        out_specs=pl.BlockSpec((1, npair, tb, LANES), lambda bi, i: (bi, 0, i, 0)),
        out_shape=jax.ShapeDtypeStruct(q.shape, BF16),
        scratch_shapes=[pltpu.VMEM((npair, pad + tb, LANES), BF16), pltpu.VMEM((npair, pad + tb, LANES), BF16),
                        pltpu.VMEM((nk, 2 * tq), F32), pltpu.VMEM((nk, 2 * tq), F32),
                        pltpu.VMEM((nk, 2 * tq), BF16), pltpu.VMEM((nk, 2 * tq), BF16),
                        pltpu.VMEM((1, 2 * tq), F32), pltpu.VMEM((1, 2 * tq), F32)],
        compiler_params=_params("arbitrary", "arbitrary"),
        name="band_attention",
    )(q, kpad, kpad, vpad, vpad, bias)


def _band_bias(rel_table, tq):
    nk = tq + BAND_PAST
    dist = np.arange(-(tq - 1), nk)
    per_dist = LOG2_E * rel_table.astype(F32)[:, np.clip(BAND_PAST - dist, -REL_CLIP, REL_CLIP) + REL_CLIP]
    span = per_dist.shape[1]
    skew = jnp.tile(per_dist, (1, tq))[:, tq - 1:tq - 1 + tq * (span - 1)].reshape(-1, tq, span - 1)
    bias = skew[:, :, :nk]
    qc, kc = np.arange(tq)[:, None] // CHUNK, np.arange(nk)[None, :] // CHUNK
    in_band = (kc >= qc) & (kc <= qc + LEFT_CHUNKS)
    bias = jnp.where(in_band[None], bias, MASK_VALUE)
    return bias.reshape(N_HEADS // 2, 2 * tq, nk).transpose(0, 2, 1)


def _router_logits(xn, wcat_ref, br_ref):
    xh = xn.astype(BF16)
    xl = (xn - xh.astype(F32)).astype(BF16)
    both = jnp.dot(xh, wcat_ref[...], preferred_element_type=F32)
    low = jnp.dot(xl, wcat_ref[:, 0:N_ROUTER], preferred_element_type=F32)
    return both[:, 0:N_ROUTER] + both[:, N_ROUTER:2 * N_ROUTER] + low + br_ref[...]


def _ffn_prologue(x1, gf_ref, wcat_ref, br_ref, tri_ref, cnt0_ref, first_step,
                  x1_ref, xn_ref, e1_ref, e2_ref, g1_ref, g2_ref, r1_ref, r2_ref, cnt_ref, carry):
    xn = _rms(x1, gf_ref[...])
    x1_ref[...] = x1
    _store_rows(xn_ref, _pack_bf16_pairs(xn))

    logits = _router_logits(xn, wcat_ref, br_ref)
    lane = lax.broadcasted_iota(I32, logits.shape, 1).astype(F32)
    is_group = lane < N_GROUPS
    neg_inf = -jnp.inf
    far = float(N_ROUTER)
    gl = jnp.where(is_group, logits, neg_inf)
    gmax = jnp.max(gl, axis=-1, keepdims=True)
    g_w = 1.0 / jnp.sum(jnp.exp(gl - gmax), axis=-1, keepdims=True)
    g_idx = jnp.min(jnp.where(gl == gmax, lane, far), axis=-1, keepdims=True)
    eidx = lane - N_GROUPS
    egroup = jnp.floor(eidx * (1.0 / EXPERTS_PER_GROUP))
    el = jnp.where(egroup == g_idx, logits, neg_inf)
    m1 = jnp.max(el, axis=-1, keepdims=True)
    i1 = jnp.min(jnp.where(el == m1, eidx, far), axis=-1, keepdims=True)
    el2 = jnp.where(eidx == i1, neg_inf, el)
    m2 = jnp.max(el2, axis=-1, keepdims=True)
    i2 = jnp.min(jnp.where(el2 == m2, eidx, far), axis=-1, keepdims=True)
    t = jnp.exp(m2 - m1)
    w_a = 1.0 / (1.0 + t)
    e1_ref[...] = i1.astype(I32)
    e2_ref[...] = i2.astype(I32)
    g1_ref[...] = g_w * w_a
    g2_ref[...] = g_w * (t * w_a)

    @pl.when(first_step)
    def _():
        carry[...] = cnt0_ref[...]

    sel1 = eidx == i1
    sel2 = eidx == i2
    onehot = jnp.where(sel1, 1.0, jnp.where(sel2, 1.0, 0.0))
    before = jnp.dot(tri_ref[...], onehot.astype(BF16), preferred_element_type=F32) + carry[...]
    r1_ref[...] = jnp.sum(jnp.where(sel1, before, 0.0), axis=-1, keepdims=True).astype(I32)
    r2_ref[...] = jnp.sum(jnp.where(sel2, before, 0.0), axis=-1, keepdims=True).astype(I32)
    carry[...] = carry[...] + jnp.sum(onehot, axis=0, keepdims=True)
    cnt_ref[...] = carry[...]


def _ffn_out(n, d, tm, row):
    nbp = d // (2 * LANES)
    col = pl.BlockSpec((tm, 1), lambda *g: (row(*g), 0))
    specs = [pl.BlockSpec((tm, d), lambda *g: (row(*g), 0)),
             pl.BlockSpec((tm * nbp, LANES), lambda *g: (row(*g), 0))] + [col] * 6 \
        + [pl.BlockSpec((1, N_ROUTER), lambda *g: (0, 0))]
    shapes = [jax.ShapeDtypeStruct((n, d), F32), jax.ShapeDtypeStruct((n * nbp, LANES), jnp.uint32),
              jax.ShapeDtypeStruct((n, 1), I32), jax.ShapeDtypeStruct((n, 1), I32),
              jax.ShapeDtypeStruct((n, 1), F32), jax.ShapeDtypeStruct((n, 1), F32),
              jax.ShapeDtypeStruct((n, 1), I32), jax.ShapeDtypeStruct((n, 1), I32),
              jax.ShapeDtypeStruct((1, N_ROUTER), F32)]
    return specs, shapes


def _strict_lower(tm):
    r = jnp.arange(tm)
    return (r[:, None] > r[None, :]).astype(BF16)


def _attn_out_kernel(a_ref, x_ref, wo_ref, gf_ref, wcat_ref, br_ref, tri_ref, cnt0_ref, *rest):
    outs, carry = rest[:-1], rest[-1]
    nbatch, npair, rows, _ = a_ref.shape
    a = jnp.concatenate([a_ref[:, hp].reshape(nbatch * rows, LANES) for hp in range(npair)], axis=-1)
    x1 = x_ref[...] + jnp.dot(a, wo_ref[...], preferred_element_type=F32)
    _ffn_prologue(x1, gf_ref, wcat_ref, br_ref, tri_ref, cnt0_ref, pl.program_id(0) == 0, *outs, carry)


def _attn_out(a, x, wo, gf, wcat, br, cnt0, *, tm):
    n, d = x.shape
    b, npair, t, _ = a.shape
    row = lambda i: (i, 0)
    fixed = lambda i: (0, 0)
    tri = _strict_lower(tm)
    specs, shapes = _ffn_out(n, d, tm, lambda i: i)
    if t >= tm:
        a_spec = pl.BlockSpec((1, npair, tm, LANES), lambda i: (i // (t // tm), 0, i % (t // tm), 0))
    else:
        a_spec = pl.BlockSpec((tm // t, npair, t, LANES), lambda i: (i, 0, 0, 0))
    return pl.pallas_call(
        _attn_out_kernel,
        grid=(n // tm,),
        in_specs=[
            a_spec, pl.BlockSpec((tm, d), row),
            pl.BlockSpec(wo.shape, fixed), pl.BlockSpec(gf.shape, fixed),
            pl.BlockSpec(wcat.shape, fixed), pl.BlockSpec(br.shape, fixed), pl.BlockSpec(tri.shape, fixed),
            pl.BlockSpec(cnt0.shape, fixed),
        ],
        out_specs=specs,
        out_shape=shapes,
        scratch_shapes=[pltpu.VMEM((1, N_ROUTER), F32)],
        compiler_params=_params("arbitrary"),
        name="attn_out_router",
    )(a, x, wo, gf, wcat, br, tri, cnt0)


def _conv_kernel(x_ref, st_ref, gm_ref, w1_ref, b1_ref, wdw_ref, bdw_ref, lg_ref, lb_ref, w2_ref, b2_ref,
                 gf_ref, wcat_ref, br_ref, tri_ref, cnt0_ref, *rest, width):
    outs, ns_ref = rest[:9], rest[9]
    ubuf, rbuf, dwbuf, carry = rest[10:]
    bi = pl.program_id(0)
    t = pl.program_id(1)
    tt = x_ref.shape[1]
    d = x_ref.shape[2]
    nb = d // LANES
    state = width - 1
    lo = CONV_HEAD - state
    shifted_rows = rbuf.shape[1]
    rb = min(CONV_ROW_BLOCK, tt)

    @pl.when(t == 0)
    def _():
        for c in range(nb):
            ubuf[c, lo:CONV_HEAD, :] = st_ref[0, :, c * LANES:(c + 1) * LANES]

    x = x_ref[0]
    xn = _rms(x, gm_ref[...]).astype(BF16)

    def glu(p):
        hid = jnp.dot(xn, w1_ref[p], preferred_element_type=F32) + b1_ref[p]
        for k in range(2):
            u = hid[:, k * LANES:(k + 1) * LANES] * jax.nn.sigmoid(hid[:, (2 + k) * LANES:(3 + k) * LANES])
            ubuf[2 * p + k, CONV_HEAD:CONV_HEAD + tt, :] = u
            ns_ref[0, 2 * p + k] = u[tt - state:tt, :]

    def depthwise(c):
        for b in range(1, SUBLANES):
            rbuf[b - 1] = ubuf[c, b:b + shifted_rows, :]
        for r in range(tt // rb):
            r0 = r * rb
            acc = jnp.broadcast_to(bdw_ref[c], (rb, LANES))
            for j in range(width):
                a8, b = divmod(lo + j, SUBLANES)
                if b == 0:
                    src = ubuf[c, lo + j + r0:lo + j + r0 + rb, :]
                else:
                    src = rbuf[b - 1, a8 * SUBLANES + r0:a8 * SUBLANES + r0 + rb, :]
                acc = acc + src * wdw_ref[c, j:j + 1, :]
            dwbuf[c, r0:r0 + rb, :] = acc
        ubuf[c, lo:CONV_HEAD, :] = ubuf[c, lo + tt:CONV_HEAD + tt, :]

    glu(0)
    for p in range(nb // 2):
        if p + 1 < nb // 2:
            glu(p + 1)
        depthwise(2 * p)
        depthwise(2 * p + 1)

    dw = jnp.concatenate([dwbuf[c] for c in range(nb)], axis=-1)
    mu = jnp.mean(dw, axis=-1, keepdims=True)
    xc = dw - mu
    z = xc * lax.rsqrt(jnp.mean(xc * xc, axis=-1, keepdims=True) + LN_EPS) * lg_ref[...] + lb_ref[...]
    z = z * jax.nn.sigmoid(z)
    x1 = x + jnp.dot(z.astype(BF16), w2_ref[...], preferred_element_type=F32) + b2_ref[...]
    first = jnp.logical_and(bi == 0, t == 0)
    _ffn_prologue(x1, gf_ref, wcat_ref, br_ref, tri_ref, cnt0_ref, first, *outs, carry)


def _conv_layer(x, state, gm, w1, b1, wdw, bdw, lg, lb, w2, b2, gf, wcat, br, cnt0, *, tt):
    b, t, d = x.shape
    width = wdw.shape[0]
    nb = d // LANES
    nt = t // tt
    wdw3 = wdw.reshape(width, nb, LANES).transpose(1, 0, 2)
    bdw3 = bdw.reshape(nb, 1, LANES)
    pair_cols = lambda m: jnp.concatenate([m[..., :d].reshape(-1, nb // 2, 2 * LANES),
                                           m[..., d:].reshape(-1, nb // 2, 2 * LANES)], axis=2).transpose(1, 0, 2)
    w1 = pair_cols(w1)
    b1 = pair_cols(b1)
    tri = _strict_lower(tt)
    fixed2 = lambda bi, ti: (0, 0)
    fixed3 = lambda bi, ti: (0, 0, 0)
    specs, shapes = _ffn_out(b * t, d, tt, lambda bi, ti: bi * nt + ti)
    shifted_rows = tt + ((CONV_HEAD - 1) // SUBLANES) * SUBLANES
    return pl.pallas_call(
        functools.partial(_conv_kernel, width=width),
        grid=(b, nt),
        in_specs=[
            pl.BlockSpec((1, tt, d), lambda bi, ti: (bi, ti, 0)),
            pl.BlockSpec((1, width - 1, d), lambda bi, ti: (bi, 0, 0)),
            pl.BlockSpec(gm.shape, fixed2), pl.BlockSpec(w1.shape, fixed3), pl.BlockSpec(b1.shape, fixed3),
            pl.BlockSpec(wdw3.shape, fixed3), pl.BlockSpec(bdw3.shape, fixed3),
            pl.BlockSpec(lg.shape, fixed2), pl.BlockSpec(lb.shape, fixed2),
            pl.BlockSpec(w2.shape, fixed2), pl.BlockSpec(b2.shape, fixed2),
            pl.BlockSpec(gf.shape, fixed2), pl.BlockSpec(wcat.shape, fixed2), pl.BlockSpec(br.shape, fixed2),
            pl.BlockSpec(tri.shape, fixed2), pl.BlockSpec(cnt0.shape, fixed2),
        ],
        out_specs=specs + [pl.BlockSpec((1, nb, width - 1, LANES), lambda bi, ti: (bi, 0, 0, 0))],
        out_shape=shapes + [jax.ShapeDtypeStruct((b, nb, width - 1, LANES), F32)],
        scratch_shapes=[
            pltpu.VMEM((nb, CONV_HEAD + tt, LANES), F32),
            pltpu.VMEM((SUBLANES - 1, shifted_rows, LANES), F32),
            pltpu.VMEM((nb, tt, LANES), F32),
            pltpu.VMEM((1, N_ROUTER), F32),
        ],
        compiler_params=_params("arbitrary", "arbitrary"),
        name="conv_module_router",
    )(x, state, gm, w1, b1, wdw3, bdw3, lg, lb, w2, b2, gf, wcat, br, tri, cnt0)


def _slot(slots_ref, tm, which, r):
    return slots_ref[0, 0, which * tm + r]


def _row_copy(src, src_row, dst, dst_row, sem, nb):
    return pltpu.make_async_copy(src.at[pl.ds(pl.multiple_of(src_row * nb, nb), nb), :],
                                 dst.at[pl.ds(pl.multiple_of(dst_row * nb, nb), nb), :], sem)


def _dispatch_kernel(starts_ref, ends_ref, meta_ref, *rest, tm, moe_tm, nb, group_steps):
    n_groups = len(group_steps)
    xn_refs = rest[:n_groups]
    xs_hbm, zbuf, sem, zsem = rest[n_groups:]
    i = pl.program_id(0)

    @pl.when(i == 0)
    def _():
        zbuf[...] = jnp.zeros_like(zbuf)

        def tile_copy(e):
            return pltpu.make_async_copy(
                zbuf, xs_hbm.at[pl.ds(pl.multiple_of((ends_ref[e] - moe_tm) * nb, nb), moe_tm * nb), :], zsem)

        for e in range(N_EXPERTS):
            @pl.when(ends_ref[e] > starts_ref[e])
            def _():
                tile_copy(e).start()
        for e in range(N_EXPERTS):
            @pl.when(ends_ref[e] > starts_ref[e])
            def _():
                tile_copy(e).wait()

        def unused_tile(j):
            return pltpu.make_async_copy(
                zbuf, xs_hbm.at[pl.ds(pl.multiple_of(j * (moe_tm * nb), moe_tm * nb), moe_tm * nb), :], zsem)

        first_unused = ends_ref[N_EXPERTS - 1] // moe_tm
        n_tiles = xs_hbm.shape[0] // (moe_tm * nb)
        lax.fori_loop(first_unused, n_tiles, lambda j, c: (unused_tile(j).start(), c)[1], 0)
        lax.fori_loop(first_unused, n_tiles, lambda j, c: (unused_tile(j).wait(), c)[1], 0)

    def scatter_tile(xn_ref):
        def copies(r):
            return [_row_copy(xn_ref, r, xs_hbm, _slot(meta_ref, tm, w, r), sem, nb) for w in range(2)]

        def issue(g, carry):
            for k in range(DMA_UNROLL):
                for queue, cp in enumerate(copies(g * DMA_UNROLL + k)):
                    cp.start(priority=queue)
            return carry

        def drain(g, carry):
            for k in range(DMA_UNROLL):
                for cp in copies(g * DMA_UNROLL + k):
                    cp.wait()
            return carry

        lax.fori_loop(0, tm // DMA_UNROLL, issue, 0)
        lax.fori_loop(0, tm // DMA_UNROLL, drain, 0)

    first = 0
    for xn_ref, steps in zip(xn_refs, group_steps):
        pl.when(jnp.logical_and(i >= first, i < first + steps))(functools.partial(scatter_tile, xn_ref))
        first += steps


def _dispatch(starts, ends, meta, xn_rows_list, *, d, tm, moe_tm, p_max):
    nb = d // (2 * LANES)
    group_steps = tuple(x.shape[0] // (nb * tm) for x in xn_rows_list)
    firsts = tuple(sum(group_steps[:k]) for k in range(len(group_steps)))

    def tile_of(k):
        return lambda i, s, e: (jnp.clip(i - firsts[k], 0, group_steps[k] - 1), 0)

    return pl.pallas_call(
        functools.partial(_dispatch_kernel, tm=tm, moe_tm=moe_tm, nb=nb, group_steps=group_steps),
        grid_spec=pltpu.PrefetchScalarGridSpec(
            num_scalar_prefetch=2,
            grid=(sum(group_steps),),
            in_specs=[pl.BlockSpec((1, 1, 2 * tm), lambda i, s, e: (i, 0, 0), memory_space=pltpu.SMEM)]
            + [pl.BlockSpec((tm * nb, LANES), tile_of(k)) for k in range(len(group_steps))],
            out_specs=pl.BlockSpec(memory_space=pl.ANY),
            scratch_shapes=[pltpu.VMEM((moe_tm * nb, LANES), jnp.uint32),
                            pltpu.SemaphoreType.DMA, pltpu.SemaphoreType.DMA],
        ),
        out_shape=jax.ShapeDtypeStruct((p_max * nb, LANES), jnp.uint32),
        compiler_params=_params("arbitrary"),
        name="moe_dispatch",
    )(starts, ends, meta, *xn_rows_list)


def _moe_kernel(te_ref, nv_ref, x_ref, wg_ref, wu_ref, wd_ref, o_ref, wgu, wdn, *, tm, nb):
    i = pl.program_id(0)
    f = wg_ref.shape[3]

    @pl.when(jnp.logical_or(i == 0, te_ref[i] != te_ref[jnp.maximum(i - 1, 0)]))
    def _():
        wgu[:, 0:f] = wg_ref[0, 0].astype(BF16)
        wgu[:, f:2 * f] = wu_ref[0, 0].astype(BF16)
        wdn[...] = wd_ref[0, 0].astype(BF16)

    @pl.when(i < nv_ref[0])
    def _():
        x = _unpack_bf16_pairs(_load_rows(x_ref, tm, nb // 2))
        hu = jnp.dot(x, wgu[...], preferred_element_type=F32)
        h, u = hu[:, 0:f], hu[:, f:2 * f]
        hid = (h * jax.nn.sigmoid(h)) * u
        _store_rows(o_ref, jnp.dot(hid.astype(BF16), wdn[...], preferred_element_type=F32))

    @pl.when(i >= nv_ref[0])
    def _():
        o_ref[...] = jnp.zeros_like(o_ref)


def _moe_grouped(tile_expert, n_valid, xs_rows, wg, wu, wd, *, layer, tm, d):
    nb = d // LANES
    p = xs_rows.shape[0] // (nb // 2)
    f = wg.shape[3]
    tile = lambda i, te, nv: (jnp.minimum(i, nv[0] - 1), 0)
    wmap = lambda i, te, nv: (layer, te[i], 0, 0)
    return pl.pallas_call(
        functools.partial(_moe_kernel, tm=tm, nb=nb),
        grid_spec=pltpu.PrefetchScalarGridSpec(
            num_scalar_prefetch=2,
            grid=(p // tm,),
            in_specs=[
                pl.BlockSpec((tm * nb // 2, LANES), tile),
                pl.BlockSpec((1, 1, d, f), wmap), pl.BlockSpec((1, 1, d, f), wmap),
                pl.BlockSpec((1, 1, f, d), wmap),
            ],
            out_specs=pl.BlockSpec((tm * nb, LANES), lambda i, te, nv: (i, 0)),
            scratch_shapes=[pltpu.VMEM((d, 2 * f), BF16), pltpu.VMEM((f, d), BF16)],
        ),
        out_shape=jax.ShapeDtypeStruct((p * nb, LANES), F32),
        compiler_params=_params("arbitrary"),
        name="moe_grouped_mlp",
    )(tile_expert, n_valid, xs_rows, wg, wu, wd)


def _combine_kernel(meta_ref, meta_next_ref, x1_ref, g1_ref, g2_ref, gfin_ref, ys_hbm, o_ref,
                    buf, sem, *, tm, nb, final_norm):
    i = pl.program_id(0)
    n_steps = pl.num_programs(0)

    def copies(meta, slot, r):
        return [_row_copy(ys_hbm, _slot(meta, tm, w, r), buf.at[slot, w], r, sem.at[slot], nb)
                for w in range(2)]

    def issue(meta, slot):
        def body(g, carry):
            for k in range(DMA_UNROLL):
                for queue, cp in enumerate(copies(meta, slot, g * DMA_UNROLL + k)):
                    cp.start(priority=queue)
            return carry
        lax.fori_loop(0, tm // DMA_UNROLL, body, 0)

    def drain(meta, slot):
        def body(g, carry):
            for k in range(DMA_UNROLL):
                for cp in copies(meta, slot, g * DMA_UNROLL + k):
                    cp.wait()
            return carry
        lax.fori_loop(0, tm // DMA_UNROLL, body, 0)

    slot = jnp.bitwise_and(i, 1)

    @pl.when(i == 0)
    def _():
        issue(meta_ref, 0)

    @pl.when(i + 1 < n_steps)
    def _():
        issue(meta_next_ref, 1 - slot)

    drain(meta_ref, slot)
    y = (x1_ref[...] + g1_ref[...] * _load_rows(buf.at[slot, 0], tm, nb)
         + g2_ref[...] * _load_rows(buf.at[slot, 1], tm, nb))
    o_ref[...] = _rms(y, gfin_ref[...]) if final_norm else y


def _combine(meta, x1, g1, g2, gfin, ys_rows, *, tm, final_norm):
    n, d = x1.shape
    nb = d // LANES
    n_steps = n // tm
    row = lambda i: (i, 0)
    return pl.pallas_call(
        functools.partial(_combine_kernel, tm=tm, nb=nb, final_norm=final_norm),
        grid=(n_steps,),
        in_specs=[
            pl.BlockSpec((1, 1, 2 * tm), lambda i: (i, 0, 0), memory_space=pltpu.SMEM),
            pl.BlockSpec((1, 1, 2 * tm), lambda i: (jnp.minimum(i + 1, n_steps - 1), 0, 0),
                         memory_space=pltpu.SMEM),
            pl.BlockSpec((tm, d), row), pl.BlockSpec((tm, 1), row), pl.BlockSpec((tm, 1), row),
            pl.BlockSpec((1, d), lambda i: (0, 0)),
            pl.BlockSpec(memory_space=pl.ANY),
        ],
        out_specs=pl.BlockSpec((tm, d), row),
        out_shape=jax.ShapeDtypeStruct((n, d), F32),
        scratch_shapes=[pltpu.VMEM((2, 2, tm * nb, LANES), F32), pltpu.SemaphoreType.DMA((2,))],
        compiler_params=_params("arbitrary"),
        name="moe_combine",
    )(meta, meta, x1, g1, g2, gfin, ys_rows)


def _moe(ffns, wg, wu, wd, gfin, *, layer, final_norm):
    d = ffns[0][0].shape[1]
    moe_tm = MOE_ROW_TILE
    n_all = sum(f[0].shape[0] for f in ffns)
    counts = ffns[-1][8][0, N_GROUPS:].astype(I32)
    padded = ((counts + moe_tm - 1) // moe_tm) * moe_tm
    ends = jnp.cumsum(padded)
    starts = ends - padded
    p_max = 2 * n_all + N_EXPERTS * moe_tm
    n_tiles = p_max // moe_tm
    n_valid = ends[-1] // moe_tm
    tile_start = jnp.arange(n_tiles, dtype=I32) * moe_tm
    tile_expert = jnp.sum((tile_start[:, None] >= ends[None, :]).astype(I32), axis=1)
    last_expert = jnp.sum((ends[-1] - 1 >= ends).astype(I32))
    tile_expert = jnp.minimum(tile_expert, last_expert)
    experts = jnp.arange(N_EXPERTS, dtype=I32)[None, :]
    slot_of = lambda e, r: r + jnp.sum(jnp.where(e == experts, starts[None, :], 0), axis=1, keepdims=True)
    tm = math.gcd(MOE_IO_TILE, *[f[0].shape[0] for f in ffns])
    metas = [jnp.concatenate([slot_of(f[2], f[6]).reshape(-1, tm), slot_of(f[3], f[7]).reshape(-1, tm)],
                             axis=1).reshape(-1, 1, 2 * tm) for f in ffns]
    xs_rows = _dispatch(starts, ends, jnp.concatenate(metas, axis=0), [f[1] for f in ffns],
                        d=d, tm=tm, moe_tm=moe_tm, p_max=p_max)
    ys_rows = _moe_grouped(tile_expert, n_valid[None], xs_rows, wg, wu, wd, layer=layer, tm=moe_tm, d=d)
    return [_combine(meta, f[0], f[4], f[5], gfin, ys_rows, tm=tm, final_norm=final_norm)
            for f, meta in zip(ffns, metas)]


def kernel(x_prompt, x_sample, cache_attn_k, cache_attn_v, cache_conv, norm_mix, norm_ffn, norm_final,
           w_qkv, w_o, rel_table, w_pw1, b_pw1, w_dw, b_dw, ln_g, ln_b, w_pw2, b_pw2,
           w_group, b_group, w_router, b_router, w_gate, w_up, w_down):
    bp, tp, d = x_prompt.shape
    bs, ts, _ = x_sample.shape
    depth = norm_mix.shape[0]
    kv_keep = cache_attn_k.shape[2]
    conv_state = cache_conv.shape[2]
    assert kv_keep == BAND_PAST and tp % ROW_TILE == 0 and ts == CHUNK and ROW_TILE == BAND_PAST
    assert w_group.shape[2] == N_GROUPS and w_router.shape[2] == N_EXPERTS
    tm_s = min(ROW_TILE, bs * ts)

    groups = ((bp, tp, ROW_TILE), (bs, ts, tm_s))
    xs_cur = [x_prompt.reshape(bp * tp, d), x_sample.reshape(bs * ts, d)]
    kv_out = [[[], []], [[], []]]
    conv_out = [[], []]
    gfin = norm_final[None, :]

    for layer in range(depth):
        gm = norm_mix[layer][None, :]
        gf = norm_ffn[layer][None, :]
        wr = jnp.concatenate([w_group[layer], w_router[layer]], axis=1)
        wr_hi = wr.astype(BF16)
        wcat = jnp.concatenate([wr_hi, (wr - wr_hi.astype(F32)).astype(BF16)], axis=1)
        br = jnp.concatenate([b_group[layer], b_router[layer]])[None, :]
        ffn = []
        cnt0 = jnp.zeros((1, N_ROUTER), F32)
        if layer % 2 == 0:
            a = layer // 2
            wqkv = w_qkv[a].astype(BF16)
            wo = w_o[a].astype(BF16)
            for gi, (b, t, tm) in enumerate(groups):
                x3 = xs_cur[gi].reshape(b, t, d)
                if gi == 0:
                    q, kpad, vpad, k32, v32 = _qkv(x3, gm, wqkv, tm=ROW_TILE, pad_blocks=1)
                    bias = _band_bias(rel_table[a], ATTN_Q_ROWS)
                    att = _attention(q, kpad, vpad, bias, tb=ROW_TILE, tq=ATTN_Q_ROWS, invalid_rows=BAND_PAST)
                    k_new, v_new = k32, v32
                else:
                    q, kn, vn, k32, v32 = _qkv(x3, gm, wqkv, tm=t, pad_blocks=0)
                    ck = cache_attn_k[a].reshape(b, kv_keep, d)
                    cv = cache_attn_v[a].reshape(b, kv_keep, d)
                    pair_major = lambda c: c.astype(BF16).reshape(b, kv_keep, d // LANES, LANES).transpose(0, 2, 1, 3)
                    kpad = jnp.concatenate([pair_major(ck), kn], axis=2)
                    vpad = jnp.concatenate([pair_major(cv), vn], axis=2)
                    bias = _band_bias(rel_table[a], t)
                    att = _attention(q, kpad, vpad, bias, tb=t, tq=t, invalid_rows=0)
                    k_new = jnp.concatenate([ck, k32], axis=1)[:, -kv_keep:]
                    v_new = jnp.concatenate([cv, v32], axis=1)[:, -kv_keep:]
                kv_out[gi][0].append(k_new.reshape(b, kv_keep, N_HEADS, d // N_HEADS))
                kv_out[gi][1].append(v_new.reshape(b, kv_keep, N_HEADS, d // N_HEADS))
                ffn.append(_attn_out(att, xs_cur[gi], wo, gf, wcat, br, cnt0, tm=tm))
                cnt0 = ffn[-1][8]
        else:
            c = layer // 2
            w1 = w_pw1[c].astype(BF16)
            w2 = w_pw2[c].astype(BF16)
            for gi, (b, t, tm) in enumerate(groups):
                x3 = xs_cur[gi].reshape(b, t, d)
                state = jnp.zeros((b, conv_state, d), F32) if gi == 0 else cache_conv[c]
                outs = _conv_layer(x3, state, gm, w1, b_pw1[c][None, :], w_dw[c], b_dw[c], ln_g[c][None, :],
                                   ln_b[c][None, :], w2, b_pw2[c][None, :], gf, wcat, br, cnt0,
                                   tt=ROW_TILE if gi == 0 else t)
                ffn.append(outs[:9])
                cnt0 = outs[8]
                conv_out[gi].append(outs[9].transpose(0, 2, 1, 3).reshape(b, conv_state, d))
        last = layer == depth - 1
        xs_cur = _moe(ffn, w_gate, w_up, w_down, gfin, layer=layer, final_norm=last)

    y_prompt = xs_cur[0].reshape(bp, tp, d)
    y_sample = xs_cur[1].reshape(bs, ts, d)
    return (y_prompt, y_sample,
            jnp.stack(kv_out[0][0]), jnp.stack(kv_out[0][1]), jnp.stack(conv_out[0]),
            jnp.stack(kv_out[1][0]), jnp.stack(kv_out[1][1]), jnp.stack(conv_out[1]))
```

```python
import functools
import math

import numpy as np
import jax
import jax.numpy as jnp
from jax import lax
from jax.experimental import pallas as pl
from jax.experimental.pallas import tpu as pltpu

F32 = jnp.float32
BF16 = jnp.bfloat16
I32 = jnp.int32

CHUNK = 64
LEFT_CHUNKS = 8
BAND_PAST = LEFT_CHUNKS * CHUNK
N_HEADS = 16
REL_CLIP = 128
N_GROUPS = 4
EXPERTS_PER_GROUP = 8
N_EXPERTS = N_GROUPS * EXPERTS_PER_GROUP
N_ROUTER = N_GROUPS + N_EXPERTS
RMS_EPS = 1e-6
LN_EPS = 1e-5
LOG2_E = 1.4426950408889634

LANES = 128
SUBLANES = 8
VMEM_LIMIT_BYTES = 56 * 1024 * 1024

MASK_VALUE = -1e30
ROW_TILE = 512
ATTN_Q_ROWS = 128
MOE_ROW_TILE = 512
MOE_IO_TILE = 1024
DMA_UNROLL = 8
CONV_HEAD = 32
CONV_ROW_BLOCK = 128


def _params(*sem):
    return pltpu.CompilerParams(dimension_semantics=sem, vmem_limit_bytes=VMEM_LIMIT_BYTES)


def _rms(x, g):
    return x * lax.rsqrt(jnp.mean(x * x, axis=-1, keepdims=True) + RMS_EPS) * g


def _store_rows(ref, val):
    rows, d = val.shape
    nb = d // LANES
    for c in range(nb):
        ref[pl.ds(c, rows, stride=nb), :] = val[:, c * LANES:(c + 1) * LANES]


def _load_rows(ref, rows, nb):
    return jnp.concatenate([ref[pl.ds(c, rows, stride=nb), :] for c in range(nb)], axis=-1)


def _pack_bf16_pairs(x):
    half = x.shape[1] // 2
    bits = pltpu.bitcast(x.astype(BF16).astype(F32), jnp.uint32)
    return jnp.right_shift(bits[:, :half], jnp.uint32(16)) | bits[:, half:]


def _unpack_bf16_pairs(w):
    lo = pltpu.bitcast(jnp.left_shift(w, jnp.uint32(16)), F32)
    hi = pltpu.bitcast(w & jnp.uint32(0xFFFF0000), F32)
    return jnp.concatenate([lo, hi], axis=-1).astype(BF16)


def _qkv_kernel(x_ref, g_ref, w32_ref, q_ref, k_ref, v_ref, k32_ref, v32_ref, w_ref, *, pad_blocks, d):
    j = pl.program_id(1)

    @pl.when(jnp.logical_and(pl.program_id(0) == 0, j == 0))
    def _():
        w_ref[...] = w32_ref[...].astype(BF16)

    @pl.when(j < pad_blocks)
    def _():
        k_ref[...] = jnp.zeros_like(k_ref)
        v_ref[...] = jnp.zeros_like(v_ref)

    @pl.when(j >= pad_blocks)
    def _():
        xn = _rms(x_ref[0], g_ref[...]).astype(BF16)
        q = jnp.dot(xn, w_ref[:, 0:d], preferred_element_type=F32) * (LOG2_E * float(d // N_HEADS) ** -0.5)
        k = jnp.dot(xn, w_ref[:, d:2 * d], preferred_element_type=F32)
        v = jnp.dot(xn, w_ref[:, 2 * d:3 * d], preferred_element_type=F32)
        k32_ref[0] = k
        v32_ref[0] = v
        for hp in range(d // LANES):
            cols = slice(hp * LANES, (hp + 1) * LANES)
            q_ref[0, hp] = q[:, cols].astype(BF16)
            k_ref[0, hp] = k[:, cols].astype(BF16)
            v_ref[0, hp] = v[:, cols].astype(BF16)


def _qkv(x, g, w, *, tm, pad_blocks):
    b, t, d = x.shape
    nt = t // tm
    npair = d // LANES
    xmap = lambda bi, j: (bi, jnp.maximum(j - pad_blocks, 0), 0)
    return pl.pallas_call(
        functools.partial(_qkv_kernel, pad_blocks=pad_blocks, d=d),
        grid=(b, nt + pad_blocks),
        in_specs=[
            pl.BlockSpec((1, tm, d), xmap),
            pl.BlockSpec((1, d), lambda bi, j: (0, 0)),
            pl.BlockSpec((d, 3 * d), lambda bi, j: (0, 0), pipeline_mode=pl.Buffered(1)),
        ],
        out_specs=[
            pl.BlockSpec((1, npair, tm, LANES), lambda bi, j: (bi, 0, jnp.maximum(j - pad_blocks, 0), 0)),
            pl.BlockSpec((1, npair, tm, LANES), lambda bi, j: (bi, 0, j, 0)),
            pl.BlockSpec((1, npair, tm, LANES), lambda bi, j: (bi, 0, j, 0)),
            pl.BlockSpec((1, tm, d), lambda bi, j: (bi, 0, 0)),
            pl.BlockSpec((1, tm, d), lambda bi, j: (bi, 0, 0)),
        ],
        out_shape=[
            jax.ShapeDtypeStruct((b, npair, t, LANES), BF16),
            jax.ShapeDtypeStruct((b, npair, t + pad_blocks * tm, LANES), BF16),
            jax.ShapeDtypeStruct((b, npair, t + pad_blocks * tm, LANES), BF16),
            jax.ShapeDtypeStruct((b, tm, d), F32),
            jax.ShapeDtypeStruct((b, tm, d), F32),
        ],
        scratch_shapes=[pltpu.VMEM((d, 3 * d), BF16)],
        compiler_params=_params("arbitrary", "arbitrary"),
        name="qkv_proj",
    )(x, g, w)


def _attn_kernel(q_ref, kp_ref, kc_ref, vp_ref, vc_ref, bias_ref, o_ref,
                 kbuf, vbuf, s_even, s_odd, p_even, p_odd, l_even, l_odd, *, tq, invalid_rows):
    i = pl.program_id(1)
    npair, tb = q_ref.shape[1], q_ref.shape[2]
    pad = kp_ref.shape[-2]
    nk = tq + pad
    n_items = (tb // tq) * npair
    assert n_items % 2 == 0 and n_items >= 4 and npair & (npair - 1) == 0
    pair_bits = npair.bit_length() - 1
    if kp_ref.ndim == 3:
        for hp in range(npair):
            kbuf[hp, 0:pad, :] = kp_ref[0, :, hp * LANES:(hp + 1) * LANES].astype(BF16)
            vbuf[hp, 0:pad, :] = vp_ref[0, :, hp * LANES:(hp + 1) * LANES].astype(BF16)
    else:
        kbuf[:, 0:pad, :] = kp_ref[0]
        vbuf[:, 0:pad, :] = vp_ref[0]
    kbuf[:, pad:pad + tb, :] = kc_ref[0]
    vbuf[:, pad:pad + tb, :] = vc_ref[0]

    key = lax.broadcasted_iota(I32, (nk, 2 * tq), 0)
    lane = lax.broadcasted_iota(I32, (1, LANES), 1)
    keep_lo = jnp.where(lane < LANES // 2, 1.0, 0.0).astype(BF16)
    keep_hi = jnp.where(lane < LANES // 2, 0.0, 1.0).astype(BF16)
    out_lo = lax.broadcasted_iota(I32, (tq, LANES), 1) < LANES // 2
    s_bufs, p_bufs, l_bufs = (s_even, s_odd), (p_even, p_odd), (l_even, l_odd)

    def where(item):
        hp = jnp.bitwise_and(item, npair - 1)
        r0 = pl.multiple_of(jnp.right_shift(item, pair_bits) * tq, tq)
        return hp, r0

    def scores(item, slot):
        hp, r0 = where(item)
        q2 = q_ref[0, hp, pl.ds(r0, tq), :]
        qm = jnp.concatenate([q2 * keep_lo, q2 * keep_hi], axis=0)
        s_bufs[slot][...] = lax.dot_general(kbuf[hp, pl.ds(r0, nk), :], qm, (((1,), (1,)), ((), ())),
                                            preferred_element_type=F32)

    def softmax(item, slot, mask_positions):
        hp, r0 = where(item)
        sc = s_bufs[slot][...] + bias_ref[hp]
        if mask_positions:
            sc = jnp.where(key >= invalid_rows - i * tb - r0, sc, MASK_VALUE)
        p = jnp.exp2(sc - jnp.max(sc, axis=0, keepdims=True))
        l_bufs[slot][...] = jnp.sum(p, axis=0, keepdims=True)
        p_bufs[slot][...] = p.astype(BF16)

    def weighted_values(item, slot):
        hp, r0 = where(item)
        o_t = lax.dot_general(vbuf[hp, pl.ds(r0, nk), :], p_bufs[slot][...], (((0,), (0,)), ((), ())),
                              preferred_element_type=F32) / l_bufs[slot][...]
        o = o_t.T
        o_ref[0, hp, pl.ds(r0, tq), :] = jnp.where(out_lo, o[0:tq], o[tq:2 * tq]).astype(o_ref.dtype)

    def run(mask_positions):
        scores(0, 0)
        scores(1, 1)
        softmax(0, 0, mask_positions)

        def two_items(g, carry):
            t = 2 * g
            scores(t, 0)
            softmax(t - 1, 1, mask_positions)
            weighted_values(t - 2, 0)
            scores(t + 1, 1)
            softmax(t, 0, mask_positions)
            weighted_values(t - 1, 1)
            return carry

        lax.fori_loop(1, n_items // 2, two_items, 0)
        softmax(n_items - 1, 1, mask_positions)
        weighted_values(n_items - 2, 0)
        weighted_values(n_items - 1, 1)

    if invalid_rows == 0:
        run(False)
    else:
        touches_invalid = i * tb < invalid_rows
        pl.when(touches_invalid)(lambda: run(True))
        pl.when(jnp.logical_not(touches_invalid))(lambda: run(False))


def _attention(q, k_past, k_cur, v_past, v_cur, bias, *, tb, tq, invalid_rows, cur0):
    b, npair, t, _ = q.shape
    pad = BAND_PAST
    nk = tq + pad
    if k_past.ndim == 3:
        past_spec = pl.BlockSpec((1, pad, k_past.shape[2]), lambda bi, i: (bi, (i * tb) // pad, 0))
    else:
        past_spec = pl.BlockSpec((1, npair, pad, LANES), lambda bi, i: (bi, 0, (i * tb) // pad, 0))
    cur_spec = pl.BlockSpec((1, npair, tb, LANES), lambda bi, i: (bi, 0, cur0 + i, 0))
    return pl.pallas_call(
        functools.partial(_attn_kernel, tq=tq, invalid_rows=invalid_rows),
        grid=(b, t // tb),
        in_specs=[
            pl.BlockSpec((1, npair, tb, LANES), lambda bi, i: (bi, 0, i, 0)),
            past_spec, cur_spec, past_spec, cur_spec,
            pl.BlockSpec(bias.shape, lambda bi, i: (0, 0, 0)),
        ],
        out_specs=pl.BlockSpec((1, npair, tb, LANES), lambda bi, i: (bi, 0, i, 0)),
        out_shape=jax.ShapeDtypeStruct(q.shape, BF16),
        scratch_shapes=[pltpu.VMEM((npair, pad + tb, LANES), BF16), pltpu.VMEM((npair, pad + tb, LANES), BF16),
                        pltpu.VMEM((nk, 2 * tq), F32), pltpu.VMEM((nk, 2 * tq), F32),
                        pltpu.VMEM((nk, 2 * tq), BF16), pltpu.VMEM((nk, 2 * tq), BF16),
                        pltpu.VMEM((1, 2 * tq), F32), pltpu.VMEM((1, 2 * tq), F32)],
        compiler_params=_params("arbitrary", "arbitrary"),
        name="band_attention",
    )(q, k_past, k_cur, v_past, v_cur, bias)


def _band_bias(rel_table, tq):
    nk = tq + BAND_PAST
    dist = np.arange(-(tq - 1), nk)
    per_dist = LOG2_E * rel_table.astype(F32)[:, np.clip(BAND_PAST - dist, -REL_CLIP, REL_CLIP) + REL_CLIP]
    span = per_dist.shape[1]
    skew = jnp.tile(per_dist, (1, tq))[:, tq - 1:tq - 1 + tq * (span - 1)].reshape(-1, tq, span - 1)
    bias = skew[:, :, :nk]
    qc, kc = np.arange(tq)[:, None] // CHUNK, np.arange(nk)[None, :] // CHUNK
    in_band = (kc >= qc) & (kc <= qc + LEFT_CHUNKS)
    bias = jnp.where(in_band[None], bias, MASK_VALUE)
    return bias.reshape(N_HEADS // 2, 2 * tq, nk).transpose(0, 2, 1)


def _router_logits(xn, wcat_ref, br_ref):
    xh = xn.astype(BF16)
    xl = (xn - xh.astype(F32)).astype(BF16)
    both = jnp.dot(xh, wcat_ref[...], preferred_element_type=F32)
    low = jnp.dot(xl, wcat_ref[:, 0:N_ROUTER], preferred_element_type=F32)
    return both[:, 0:N_ROUTER] + both[:, N_ROUTER:2 * N_ROUTER] + low + br_ref[...]


def _ffn_prologue(x1, gf_ref, wcat_ref, br_ref, tri_ref, cnt0_ref, first_step,
                  x1_ref, xn_ref, e1_ref, e2_ref, g1_ref, g2_ref, r1_ref, r2_ref, cnt_ref, carry):
    xn = _rms(x1, gf_ref[...])
    x1_ref[...] = x1
    _store_rows(xn_ref, _pack_bf16_pairs(xn))

    logits = _router_logits(xn, wcat_ref, br_ref)
    lane = lax.broadcasted_iota(I32, logits.shape, 1).astype(F32)
    is_group = lane < N_GROUPS
    neg_inf = -jnp.inf
    far = float(N_ROUTER)
    gl = jnp.where(is_group, logits, neg_inf)
    gmax = jnp.max(gl, axis=-1, keepdims=True)
    g_w = 1.0 / jnp.sum(jnp.exp(gl - gmax), axis=-1, keepdims=True)
    g_idx = jnp.min(jnp.where(gl == gmax, lane, far), axis=-1, keepdims=True)
    eidx = lane - N_GROUPS
    egroup = jnp.floor(eidx * (1.0 / EXPERTS_PER_GROUP))
    el = jnp.where(egroup == g_idx, logits, neg_inf)
    m1 = jnp.max(el, axis=-1, keepdims=True)
    i1 = jnp.min(jnp.where(el == m1, eidx, far), axis=-1, keepdims=True)
    el2 = jnp.where(eidx == i1, neg_inf, el)
    m2 = jnp.max(el2, axis=-1, keepdims=True)
    i2 = jnp.min(jnp.where(el2 == m2, eidx, far), axis=-1, keepdims=True)
    t = jnp.exp(m2 - m1)
    w_a = 1.0 / (1.0 + t)
    e1_ref[...] = i1.astype(I32)
    e2_ref[...] = i2.astype(I32)
    g1_ref[...] = g_w * w_a
    g2_ref[...] = g_w * (t * w_a)

    @pl.when(first_step)
    def _():
        carry[...] = cnt0_ref[...]

    sel1 = eidx == i1
    sel2 = eidx == i2
    onehot = jnp.where(sel1, 1.0, jnp.where(sel2, 1.0, 0.0))
    before = jnp.dot(tri_ref[...], onehot.astype(BF16), preferred_element_type=F32) + carry[...]
    r1_ref[...] = jnp.sum(jnp.where(sel1, before, 0.0), axis=-1, keepdims=True).astype(I32)
    r2_ref[...] = jnp.sum(jnp.where(sel2, before, 0.0), axis=-1, keepdims=True).astype(I32)
    carry[...] = carry[...] + jnp.sum(onehot, axis=0, keepdims=True)
    cnt_ref[...] = carry[...]


def _ffn_out(n, d, tm, row):
    nbp = d // (2 * LANES)
    col = pl.BlockSpec((tm, 1), lambda *g: (row(*g), 0))
    specs = [pl.BlockSpec((tm, d), lambda *g: (row(*g), 0)),
             pl.BlockSpec((tm * nbp, LANES), lambda *g: (row(*g), 0))] + [col] * 6 \
        + [pl.BlockSpec((1, N_ROUTER), lambda *g: (0, 0))]
    shapes = [jax.ShapeDtypeStruct((n, d), F32), jax.ShapeDtypeStruct((n * nbp, LANES), jnp.uint32),
              jax.ShapeDtypeStruct((n, 1), I32), jax.ShapeDtypeStruct((n, 1), I32),
              jax.ShapeDtypeStruct((n, 1), F32), jax.ShapeDtypeStruct((n, 1), F32),
              jax.ShapeDtypeStruct((n, 1), I32), jax.ShapeDtypeStruct((n, 1), I32),
              jax.ShapeDtypeStruct((1, N_ROUTER), F32)]
    return specs, shapes


def _strict_lower(tm):
    r = jnp.arange(tm)
    return (r[:, None] > r[None, :]).astype(BF16)


def _attn_out_kernel(a_ref, x_ref, wo32_ref, gf_ref, wcat_ref, br_ref, tri_ref, cnt0_ref, *rest):
    outs, carry, wo_ref = rest[:-2], rest[-2], rest[-1]

    @pl.when(pl.program_id(0) == 0)
    def _():
        wo_ref[...] = wo32_ref[...].astype(BF16)

    nbatch, npair, rows, _ = a_ref.shape
    a = jnp.concatenate([a_ref[:, hp].reshape(nbatch * rows, LANES) for hp in range(npair)], axis=-1)
    x1 = x_ref[...] + jnp.dot(a, wo_ref[...], preferred_element_type=F32)
    _ffn_prologue(x1, gf_ref, wcat_ref, br_ref, tri_ref, cnt0_ref, pl.program_id(0) == 0, *outs, carry)


def _attn_out(a, x, wo, gf, wcat, br, cnt0, *, tm):
    n, d = x.shape
    b, npair, t, _ = a.shape
    row = lambda i: (i, 0)
    fixed = lambda i: (0, 0)
    tri = _strict_lower(tm)
    specs, shapes = _ffn_out(n, d, tm, lambda i: i)
    if t >= tm:
        a_spec = pl.BlockSpec((1, npair, tm, LANES), lambda i: (i // (t // tm), 0, i % (t // tm), 0))
    else:
        a_spec = pl.BlockSpec((tm // t, npair, t, LANES), lambda i: (i, 0, 0, 0))
    return pl.pallas_call(
        _attn_out_kernel,
        grid=(n // tm,),
        in_specs=[
            a_spec, pl.BlockSpec((tm, d), row),
            pl.BlockSpec(wo.shape, fixed, pipeline_mode=pl.Buffered(1)), pl.BlockSpec(gf.shape, fixed),
            pl.BlockSpec(wcat.shape, fixed), pl.BlockSpec(br.shape, fixed), pl.BlockSpec(tri.shape, fixed),
            pl.BlockSpec(cnt0.shape, fixed),
        ],
        out_specs=specs,
        out_shape=shapes,
        scratch_shapes=[pltpu.VMEM((1, N_ROUTER), F32), pltpu.VMEM(wo.shape, BF16)],
        compiler_params=_params("arbitrary"),
        name="attn_out_router",
    )(a, x, wo, gf, wcat, br, tri, cnt0)


def _conv_kernel(x_ref, st_ref, gm_ref, w1_ref, b1_ref, wdw_ref, bdw_ref, lg_ref, lb_ref, w2_ref, b2_ref,
                 gf_ref, wcat_ref, br_ref, tri_ref, cnt0_ref, *rest, width):
    outs, ns_ref = rest[:9], rest[9]
    ubuf, rbuf, dwbuf, carry = rest[10:]
    bi = pl.program_id(0)
    t = pl.program_id(1)
    tt = x_ref.shape[1]
    d = x_ref.shape[2]
    nb = d // LANES
    state = width - 1
    lo = CONV_HEAD - state
    shifted_rows = rbuf.shape[1]
    rb = min(CONV_ROW_BLOCK, tt)

    @pl.when(t == 0)
    def _():
        for c in range(nb):
            ubuf[c, lo:CONV_HEAD, :] = st_ref[0, :, c * LANES:(c + 1) * LANES]

    x = x_ref[0]
    xn = _rms(x, gm_ref[...]).astype(BF16)

    def glu(p):
        hid = jnp.dot(xn, w1_ref[p], preferred_element_type=F32) + b1_ref[p]
        for k in range(2):
            u = hid[:, k * LANES:(k + 1) * LANES] * jax.nn.sigmoid(hid[:, (2 + k) * LANES:(3 + k) * LANES])
            ubuf[2 * p + k, CONV_HEAD:CONV_HEAD + tt, :] = u
            ns_ref[0, 2 * p + k] = u[tt - state:tt, :]

    def depthwise(c):
        for b in range(1, SUBLANES):
            rbuf[b - 1] = ubuf[c, b:b + shifted_rows, :]
        for r in range(tt // rb):
            r0 = r * rb
            acc = jnp.broadcast_to(bdw_ref[c], (rb, LANES))
            for j in range(width):
                a8, b = divmod(lo + j, SUBLANES)
                if b == 0:
                    src = ubuf[c, lo + j + r0:lo + j + r0 + rb, :]
                else:
                    src = rbuf[b - 1, a8 * SUBLANES + r0:a8 * SUBLANES + r0 + rb, :]
                acc = acc + src * wdw_ref[c, j:j + 1, :]
            dwbuf[c, r0:r0 + rb, :] = acc
        ubuf[c, lo:CONV_HEAD, :] = ubuf[c, lo + tt:CONV_HEAD + tt, :]

    glu(0)
    for p in range(nb // 2):
        if p + 1 < nb // 2:
            glu(p + 1)
        depthwise(2 * p)
        depthwise(2 * p + 1)

    dw = jnp.concatenate([dwbuf[c] for c in range(nb)], axis=-1)
    mu = jnp.mean(dw, axis=-1, keepdims=True)
    xc = dw - mu
    z = xc * lax.rsqrt(jnp.mean(xc * xc, axis=-1, keepdims=True) + LN_EPS) * lg_ref[...] + lb_ref[...]
    z = z * jax.nn.sigmoid(z)
    x1 = x + jnp.dot(z.astype(BF16), w2_ref[...], preferred_element_type=F32) + b2_ref[...]
    first = jnp.logical_and(bi == 0, t == 0)
    _ffn_prologue(x1, gf_ref, wcat_ref, br_ref, tri_ref, cnt0_ref, first, *outs, carry)


def _conv_layer(x, state, gm, w1, b1, wdw, bdw, lg, lb, w2, b2, gf, wcat, br, cnt0, *, tt):
    b, t, d = x.shape
    width = wdw.shape[0]
    nb = d // LANES
    nt = t // tt
    wdw3 = wdw.reshape(width, nb, LANES).transpose(1, 0, 2)
    bdw3 = bdw.reshape(nb, 1, LANES)
    pair_cols = lambda m: jnp.concatenate([m[..., :d].reshape(-1, nb // 2, 2 * LANES),
                                           m[..., d:].reshape(-1, nb // 2, 2 * LANES)], axis=2).transpose(1, 0, 2)
    w1 = pair_cols(w1)
    b1 = pair_cols(b1)
    tri = _strict_lower(tt)
    fixed2 = lambda bi, ti: (0, 0)
    fixed3 = lambda bi, ti: (0, 0, 0)
    specs, shapes = _ffn_out(b * t, d, tt, lambda bi, ti: bi * nt + ti)
    shifted_rows = tt + ((CONV_HEAD - 1) // SUBLANES) * SUBLANES
    return pl.pallas_call(
        functools.partial(_conv_kernel, width=width),
        grid=(b, nt),
        in_specs=[
            pl.BlockSpec((1, tt, d), lambda bi, ti: (bi, ti, 0)),
            pl.BlockSpec((1, width - 1, d), lambda bi, ti: (bi, 0, 0)),
            pl.BlockSpec(gm.shape, fixed2), pl.BlockSpec(w1.shape, fixed3), pl.BlockSpec(b1.shape, fixed3),
            pl.BlockSpec(wdw3.shape, fixed3), pl.BlockSpec(bdw3.shape, fixed3),
            pl.BlockSpec(lg.shape, fixed2), pl.BlockSpec(lb.shape, fixed2),
            pl.BlockSpec(w2.shape, fixed2), pl.BlockSpec(b2.shape, fixed2),
            pl.BlockSpec(gf.shape, fixed2), pl.BlockSpec(wcat.shape, fixed2), pl.BlockSpec(br.shape, fixed2),
            pl.BlockSpec(tri.shape, fixed2), pl.BlockSpec(cnt0.shape, fixed2),
        ],
        out_specs=specs + [pl.BlockSpec((1, nb, width - 1, LANES), lambda bi, ti: (bi, 0, 0, 0))],
        out_shape=shapes + [jax.ShapeDtypeStruct((b, nb, width - 1, LANES), F32)],
        scratch_shapes=[
            pltpu.VMEM((nb, CONV_HEAD + tt, LANES), F32),
            pltpu.VMEM((SUBLANES - 1, shifted_rows, LANES), F32),
            pltpu.VMEM((nb, tt, LANES), F32),
            pltpu.VMEM((1, N_ROUTER), F32),
        ],
        compiler_params=_params("arbitrary", "arbitrary"),
        name="conv_module_router",
    )(x, state, gm, w1, b1, wdw3, bdw3, lg, lb, w2, b2, gf, wcat, br, tri, cnt0)


def _slot(slots_ref, tm, which, r):
    return slots_ref[0, 0, which * tm + r]


def _row_copy(src, src_row, dst, dst_row, sem, nb):
    return pltpu.make_async_copy(src.at[pl.ds(pl.multiple_of(src_row * nb, nb), nb), :],
                                 dst.at[pl.ds(pl.multiple_of(dst_row * nb, nb), nb), :], sem)


def _dispatch_kernel(starts_ref, ends_ref, meta_ref, *rest, tm, moe_tm, nb, group_steps):
    n_groups = len(group_steps)
    xn_refs = rest[:n_groups]
    xs_hbm, zbuf, sem, zsem = rest[n_groups:]
    i = pl.program_id(0)

    @pl.when(i == 0)
    def _():
        zbuf[...] = jnp.zeros_like(zbuf)

        def tile_copy(e):
            return pltpu.make_async_copy(
                zbuf, xs_hbm.at[pl.ds(pl.multiple_of((ends_ref[e] - moe_tm) * nb, nb), moe_tm * nb), :], zsem)

        for e in range(N_EXPERTS):
            @pl.when(ends_ref[e] > starts_ref[e])
            def _():
                tile_copy(e).start()
        for e in range(N_EXPERTS):
            @pl.when(ends_ref[e] > starts_ref[e])
            def _():
                tile_copy(e).wait()

        def unused_tile(j):
            return pltpu.make_async_copy(
                zbuf, xs_hbm.at[pl.ds(pl.multiple_of(j * (moe_tm * nb), moe_tm * nb), moe_tm * nb), :], zsem)

        first_unused = ends_ref[N_EXPERTS - 1] // moe_tm
        n_tiles = xs_hbm.shape[0] // (moe_tm * nb)
        lax.fori_loop(first_unused, n_tiles, lambda j, c: (unused_tile(j).start(), c)[1], 0)
        lax.fori_loop(first_unused, n_tiles, lambda j, c: (unused_tile(j).wait(), c)[1], 0)

    def scatter_tile(xn_ref):
        def copies(r):
            return [_row_copy(xn_ref, r, xs_hbm, _slot(meta_ref, tm, w, r), sem, nb) for w in range(2)]

        def issue(g, carry):
            for k in range(DMA_UNROLL):
                for queue, cp in enumerate(copies(g * DMA_UNROLL + k)):
                    cp.start(priority=queue)
            return carry

        def drain(g, carry):
            for k in range(DMA_UNROLL):
                for cp in copies(g * DMA_UNROLL + k):
                    cp.wait()
            return carry

        lax.fori_loop(0, tm // DMA_UNROLL, issue, 0)
        lax.fori_loop(0, tm // DMA_UNROLL, drain, 0)

    first = 0
    for xn_ref, steps in zip(xn_refs, group_steps):
        pl.when(jnp.logical_and(i >= first, i < first + steps))(functools.partial(scatter_tile, xn_ref))
        first += steps


def _dispatch(starts, ends, meta, xn_rows_list, *, d, tm, moe_tm, p_max):
    nb = d // (2 * LANES)
    group_steps = tuple(x.shape[0] // (nb * tm) for x in xn_rows_list)
    firsts = tuple(sum(group_steps[:k]) for k in range(len(group_steps)))

    def tile_of(k):
        return lambda i, s, e: (jnp.clip(i - firsts[k], 0, group_steps[k] - 1), 0)

    return pl.pallas_call(
        functools.partial(_dispatch_kernel, tm=tm, moe_tm=moe_tm, nb=nb, group_steps=group_steps),
        grid_spec=pltpu.PrefetchScalarGridSpec(
            num_scalar_prefetch=2,
            grid=(sum(group_steps),),
            in_specs=[pl.BlockSpec((1, 1, 2 * tm), lambda i, s, e: (i, 0, 0), memory_space=pltpu.SMEM)]
            + [pl.BlockSpec((tm * nb, LANES), tile_of(k)) for k in range(len(group_steps))],
            out_specs=pl.BlockSpec(memory_space=pl.ANY),
            scratch_shapes=[pltpu.VMEM((moe_tm * nb, LANES), jnp.uint32),
                            pltpu.SemaphoreType.DMA, pltpu.SemaphoreType.DMA],
        ),
        out_shape=jax.ShapeDtypeStruct((p_max * nb, LANES), jnp.uint32),
        compiler_params=_params("arbitrary"),
        name="moe_dispatch",
    )(starts, ends, meta, *xn_rows_list)


def _moe_kernel(te_ref, nv_ref, x_ref, wg_ref, wu_ref, wd_ref, o_ref, wgu, wdn, *, tm, nb):
    i = pl.program_id(0)
    f = wg_ref.shape[3]

    @pl.when(jnp.logical_or(i == 0, te_ref[i] != te_ref[jnp.maximum(i - 1, 0)]))
    def _():
        wgu[:, 0:f] = wg_ref[0, 0].astype(BF16)
        wgu[:, f:2 * f] = wu_ref[0, 0].astype(BF16)
        wdn[...] = wd_ref[0, 0].astype(BF16)

    @pl.when(i < nv_ref[0])
    def _():
        x = _unpack_bf16_pairs(_load_rows(x_ref, tm, nb // 2))
        hu = jnp.dot(x, wgu[...], preferred_element_type=F32)
        h, u = hu[:, 0:f], hu[:, f:2 * f]
        hid = (h * jax.nn.sigmoid(h)) * u
        _store_rows(o_ref, jnp.dot(hid.astype(BF16), wdn[...], preferred_element_type=F32))

    @pl.when(i >= nv_ref[0])
    def _():
        o_ref[...] = jnp.zeros_like(o_ref)


def _moe_grouped(tile_expert, n_valid, xs_rows, wg, wu, wd, *, layer, tm, d):
    nb = d // LANES
    p = xs_rows.shape[0] // (nb // 2)
    f = wg.shape[3]
    tile = lambda i, te, nv: (jnp.minimum(i, nv[0] - 1), 0)
    wmap = lambda i, te, nv: (layer, te[i], 0, 0)
    return pl.pallas_call(
        functools.partial(_moe_kernel, tm=tm, nb=nb),
        grid_spec=pltpu.PrefetchScalarGridSpec(
            num_scalar_prefetch=2,
            grid=(p // tm,),
            in_specs=[
                pl.BlockSpec((tm * nb // 2, LANES), tile),
                pl.BlockSpec((1, 1, d, f), wmap), pl.BlockSpec((1, 1, d, f), wmap),
                pl.BlockSpec((1, 1, f, d), wmap),
            ],
            out_specs=pl.BlockSpec((tm * nb, LANES), lambda i, te, nv: (i, 0)),
            scratch_shapes=[pltpu.VMEM((d, 2 * f), BF16), pltpu.VMEM((f, d), BF16)],
        ),
        out_shape=jax.ShapeDtypeStruct((p * nb, LANES), F32),
        compiler_params=_params("arbitrary"),
        name="moe_grouped_mlp",
    )(tile_expert, n_valid, xs_rows, wg, wu, wd)


def _combine_kernel(meta_ref, meta_next_ref, x1_ref, g1_ref, g2_ref, gfin_ref, ys_hbm, o_ref,
                    buf, sem, *, tm, nb, final_norm):
    i = pl.program_id(0)
    n_steps = pl.num_programs(0)

    def copies(meta, slot, r):
        return [_row_copy(ys_hbm, _slot(meta, tm, w, r), buf.at[slot, w], r, sem.at[slot], nb)
                for w in range(2)]

    def issue(meta, slot):
        def body(g, carry):
            for k in range(DMA_UNROLL):
                for queue, cp in enumerate(copies(meta, slot, g * DMA_UNROLL + k)):
                    cp.start(priority=queue)
            return carry
        lax.fori_loop(0, tm // DMA_UNROLL, body, 0)

    def drain(meta, slot):
        def body(g, carry):
            for k in range(DMA_UNROLL):
                for cp in copies(meta, slot, g * DMA_UNROLL + k):
                    cp.wait()
            return carry
        lax.fori_loop(0, tm // DMA_UNROLL, body, 0)

    slot = jnp.bitwise_and(i, 1)

    @pl.when(i == 0)
    def _():
        issue(meta_ref, 0)

    @pl.when(i + 1 < n_steps)
    def _():
        issue(meta_next_ref, 1 - slot)

    drain(meta_ref, slot)
    y = (x1_ref[...] + g1_ref[...] * _load_rows(buf.at[slot, 0], tm, nb)
         + g2_ref[...] * _load_rows(buf.at[slot, 1], tm, nb))
    o_ref[...] = _rms(y, gfin_ref[...]) if final_norm else y


def _combine(meta, x1, g1, g2, gfin, ys_rows, *, tm, final_norm):
    n, d = x1.shape
    nb = d // LANES
    n_steps = n // tm
    row = lambda i: (i, 0)
    return pl.pallas_call(
        functools.partial(_combine_kernel, tm=tm, nb=nb, final_norm=final_norm),
        grid=(n_steps,),
        in_specs=[
            pl.BlockSpec((1, 1, 2 * tm), lambda i: (i, 0, 0), memory_space=pltpu.SMEM),
            pl.BlockSpec((1, 1, 2 * tm), lambda i: (jnp.minimum(i + 1, n_steps - 1), 0, 0),
                         memory_space=pltpu.SMEM),
            pl.BlockSpec((tm, d), row), pl.BlockSpec((tm, 1), row), pl.BlockSpec((tm, 1), row),
            pl.BlockSpec((1, d), lambda i: (0, 0)),
            pl.BlockSpec(memory_space=pl.ANY),
        ],
        out_specs=pl.BlockSpec((tm, d), row),
        out_shape=jax.ShapeDtypeStruct((n, d), F32),
        scratch_shapes=[pltpu.VMEM((2, 2, tm * nb, LANES), F32), pltpu.SemaphoreType.DMA((2,))],
        compiler_params=_params("arbitrary"),
        name="moe_combine",
    )(meta, meta, x1, g1, g2, gfin, ys_rows)


def _moe(ffns, wg, wu, wd, gfin, *, layer, final_norm):
    d = ffns[0][0].shape[1]
    moe_tm = MOE_ROW_TILE
    n_all = sum(f[0].shape[0] for f in ffns)
    counts = ffns[-1][8][0, N_GROUPS:].astype(I32)
    padded = ((counts + moe_tm - 1) // moe_tm) * moe_tm
    ends = jnp.cumsum(padded)
    starts = ends - padded
    p_max = 2 * n_all + N_EXPERTS * moe_tm
    n_tiles = p_max // moe_tm
    n_valid = ends[-1] // moe_tm
    tile_start = jnp.arange(n_tiles, dtype=I32) * moe_tm
    tile_expert = jnp.sum((tile_start[:, None] >= ends[None, :]).astype(I32), axis=1)
    last_expert = jnp.sum((ends[-1] - 1 >= ends).astype(I32))
    tile_expert = jnp.minimum(tile_expert, last_expert)
    experts = jnp.arange(N_EXPERTS, dtype=I32)[None, :]
    slot_of = lambda e, r: r + jnp.sum(jnp.where(e == experts, starts[None, :], 0), axis=1, keepdims=True)
    tm = math.gcd(MOE_IO_TILE, *[f[0].shape[0] for f in ffns])
    metas = [jnp.concatenate([slot_of(f[2], f[6]).reshape(-1, tm), slot_of(f[3], f[7]).reshape(-1, tm)],
                             axis=1).reshape(-1, 1, 2 * tm) for f in ffns]
    xs_rows = _dispatch(starts, ends, jnp.concatenate(metas, axis=0), [f[1] for f in ffns],
                        d=d, tm=tm, moe_tm=moe_tm, p_max=p_max)
    ys_rows = _moe_grouped(tile_expert, n_valid[None], xs_rows, wg, wu, wd, layer=layer, tm=moe_tm, d=d)
    return [_combine(meta, f[0], f[4], f[5], gfin, ys_rows, tm=tm, final_norm=final_norm)
            for f, meta in zip(ffns, metas)]


def kernel(x_prompt, x_sample, cache_attn_k, cache_attn_v, cache_conv, norm_mix, norm_ffn, norm_final,
           w_qkv, w_o, rel_table, w_pw1, b_pw1, w_dw, b_dw, ln_g, ln_b, w_pw2, b_pw2,
           w_group, b_group, w_router, b_router, w_gate, w_up, w_down):
    bp, tp, d = x_prompt.shape
    bs, ts, _ = x_sample.shape
    depth = norm_mix.shape[0]
    kv_keep = cache_attn_k.shape[2]
    conv_state = cache_conv.shape[2]
    assert kv_keep == BAND_PAST and tp % ROW_TILE == 0 and ts == CHUNK and ROW_TILE == BAND_PAST
    assert w_group.shape[2] == N_GROUPS and w_router.shape[2] == N_EXPERTS
    tm_s = min(ROW_TILE, bs * ts)

    groups = ((bp, tp, ROW_TILE), (bs, ts, tm_s))
    xs_cur = [x_prompt.reshape(bp * tp, d), x_sample.reshape(bs * ts, d)]
    kv_out = [[[], []], [[], []]]
    conv_out = [[], []]
    gfin = norm_final[None, :]

    for layer in range(depth):
        gm = norm_mix[layer][None, :]
        gf = norm_ffn[layer][None, :]
        wr = jnp.concatenate([w_group[layer], w_router[layer]], axis=1)
        wr_hi = wr.astype(BF16)
        wcat = jnp.concatenate([wr_hi, (wr - wr_hi.astype(F32)).astype(BF16)], axis=1)
        br = jnp.concatenate([b_group[layer], b_router[layer]])[None, :]
        ffn = []
        cnt0 = jnp.zeros((1, N_ROUTER), F32)
        if layer % 2 == 0:
            a = layer // 2
            wqkv = w_qkv[a]
            wo = w_o[a]
            for gi, (b, t, tm) in enumerate(groups):
                x3 = xs_cur[gi].reshape(b, t, d)
                if gi == 0:
                    q, kpad, vpad, k32, v32 = _qkv(x3, gm, wqkv, tm=ROW_TILE, pad_blocks=1)
                    bias = _band_bias(rel_table[a], ATTN_Q_ROWS)
                    att = _attention(q, kpad, kpad, vpad, vpad, bias, tb=ROW_TILE, tq=ATTN_Q_ROWS,
                                     invalid_rows=BAND_PAST, cur0=BAND_PAST // ROW_TILE)
                    k_new, v_new = k32, v32
                else:
                    q, kn, vn, k32, v32 = _qkv(x3, gm, wqkv, tm=t, pad_blocks=0)
                    ck = cache_attn_k[a].reshape(b, kv_keep, d)
                    cv = cache_attn_v[a].reshape(b, kv_keep, d)
                    bias = _band_bias(rel_table[a], t)
                    att = _attention(q, ck, kn, cv, vn, bias, tb=t, tq=t, invalid_rows=0, cur0=0)
                    k_new = jnp.concatenate([ck, k32], axis=1)[:, -kv_keep:]
                    v_new = jnp.concatenate([cv, v32], axis=1)[:, -kv_keep:]
                kv_out[gi][0].append(k_new.reshape(b, kv_keep, N_HEADS, d // N_HEADS))
                kv_out[gi][1].append(v_new.reshape(b, kv_keep, N_HEADS, d // N_HEADS))
                ffn.append(_attn_out(att, xs_cur[gi], wo, gf, wcat, br, cnt0, tm=tm))
                cnt0 = ffn[-1][8]
        else:
            c = layer // 2
            w1 = w_pw1[c].astype(BF16)
            w2 = w_pw2[c].astype(BF16)
            for gi, (b, t, tm) in enumerate(groups):
                x3 = xs_cur[gi].reshape(b, t, d)
                state = jnp.zeros((b, conv_state, d), F32) if gi == 0 else cache_conv[c]
                outs = _conv_layer(x3, state, gm, w1, b_pw1[c][None, :], w_dw[c], b_dw[c], ln_g[c][None, :],
                                   ln_b[c][None, :], w2, b_pw2[c][None, :], gf, wcat, br, cnt0,
                                   tt=ROW_TILE if gi == 0 else t)
                ffn.append(outs[:9])
                cnt0 = outs[8]
                conv_out[gi].append(outs[9].transpose(0, 2, 1, 3).reshape(b, conv_state, d))
        last = layer == depth - 1
        xs_cur = _moe(ffn, w_gate, w_up, w_down, gfin, layer=layer, final_norm=last)

    y_prompt = xs_cur[0].reshape(bp, tp, d)
    y_sample = xs_cur[1].reshape(bs, ts, d)
    return (y_prompt, y_sample,
            jnp.stack(kv_out[0][0]), jnp.stack(kv_out[0][1]), jnp.stack(conv_out[0]),
            jnp.stack(kv_out[1][0]), jnp.stack(kv_out[1][1]), jnp.stack(conv_out[1]))
```

```python
import functools
import math

import numpy as np
import jax
import jax.numpy as jnp
from jax import lax
from jax.experimental import pallas as pl
from jax.experimental.pallas import tpu as pltpu

F32 = jnp.float32
BF16 = jnp.bfloat16
I32 = jnp.int32

CHUNK = 64
LEFT_CHUNKS = 8
BAND_PAST = LEFT_CHUNKS * CHUNK
N_HEADS = 16
REL_CLIP = 128
N_GROUPS = 4
EXPERTS_PER_GROUP = 8
N_EXPERTS = N_GROUPS * EXPERTS_PER_GROUP
N_ROUTER = N_GROUPS + N_EXPERTS
RMS_EPS = 1e-6
LN_EPS = 1e-5
LOG2_E = 1.4426950408889634

LANES = 128
SUBLANES = 8
VMEM_LIMIT_BYTES = 56 * 1024 * 1024

MASK_VALUE = -1e30
ROW_TILE = 512
ATTN_Q_ROWS = 128
MOE_ROW_TILE = 512
MOE_IO_TILE = 1024
DMA_UNROLL = 8
CONV_HEAD = 32
CONV_ROW_BLOCK = 128


def _params(*sem):
    return pltpu.CompilerParams(dimension_semantics=sem, vmem_limit_bytes=VMEM_LIMIT_BYTES)


def _rms(x, g):
    return x * lax.rsqrt(jnp.mean(x * x, axis=-1, keepdims=True) + RMS_EPS) * g


def _store_rows(ref, val):
    rows, d = val.shape
    nb = d // LANES
    for c in range(nb):
        ref[pl.ds(c, rows, stride=nb), :] = val[:, c * LANES:(c + 1) * LANES]


def _load_rows(ref, rows, nb):
    return jnp.concatenate([ref[pl.ds(c, rows, stride=nb), :] for c in range(nb)], axis=-1)


def _pack_bf16_pairs(x):
    half = x.shape[1] // 2
    bits = pltpu.bitcast(x.astype(BF16).astype(F32), jnp.uint32)
    return jnp.right_shift(bits[:, :half], jnp.uint32(16)) | bits[:, half:]


def _unpack_bf16_pairs(w):
    lo = pltpu.bitcast(jnp.left_shift(w, jnp.uint32(16)), F32)
    hi = pltpu.bitcast(w & jnp.uint32(0xFFFF0000), F32)
    return jnp.concatenate([lo, hi], axis=-1).astype(BF16)


def _qkv_kernel(x_ref, g_ref, w32_ref, q_ref, k_ref, v_ref, k32_ref, v32_ref, w_ref, *, pad_blocks, d):
    j = pl.program_id(1)

    @pl.when(jnp.logical_and(pl.program_id(0) == 0, j == 0))
    def _():
        w_ref[...] = w32_ref[...].astype(BF16)

    @pl.when(j < pad_blocks)
    def _():
        k_ref[...] = jnp.zeros_like(k_ref)
        v_ref[...] = jnp.zeros_like(v_ref)

    @pl.when(j >= pad_blocks)
    def _():
        xn = _rms(x_ref[0], g_ref[...]).astype(BF16)
        q = jnp.dot(xn, w_ref[:, 0:d], preferred_element_type=F32) * (LOG2_E * float(d // N_HEADS) ** -0.5)
        k = jnp.dot(xn, w_ref[:, d:2 * d], preferred_element_type=F32)
        v = jnp.dot(xn, w_ref[:, 2 * d:3 * d], preferred_element_type=F32)
        k32_ref[0] = k
        v32_ref[0] = v
        for hp in range(d // LANES):
            cols = slice(hp * LANES, (hp + 1) * LANES)
            q_ref[0, hp] = q[:, cols].astype(BF16)
            k_ref[0, hp] = k[:, cols].astype(BF16)
            v_ref[0, hp] = v[:, cols].astype(BF16)


def _qkv(x, g, w, *, tm, pad_blocks):
    b, t, d = x.shape
    nt = t // tm
    npair = d // LANES
    xmap = lambda bi, j: (bi, jnp.maximum(j - pad_blocks, 0), 0)
    return pl.pallas_call(
        functools.partial(_qkv_kernel, pad_blocks=pad_blocks, d=d),
        grid=(b, nt + pad_blocks),
        in_specs=[
            pl.BlockSpec((1, tm, d), xmap),
            pl.BlockSpec((1, d), lambda bi, j: (0, 0)),
            pl.BlockSpec((d, 3 * d), lambda bi, j: (0, 0), pipeline_mode=pl.Buffered(1)),
        ],
        out_specs=[
            pl.BlockSpec((1, npair, tm, LANES), lambda bi, j: (bi, 0, jnp.maximum(j - pad_blocks, 0), 0)),
            pl.BlockSpec((1, npair, tm, LANES), lambda bi, j: (bi, 0, j, 0)),
            pl.BlockSpec((1, npair, tm, LANES), lambda bi, j: (bi, 0, j, 0)),
            pl.BlockSpec((1, tm, d), lambda bi, j: (bi, 0, 0)),
            pl.BlockSpec((1, tm, d), lambda bi, j: (bi, 0, 0)),
        ],
        out_shape=[
            jax.ShapeDtypeStruct((b, npair, t, LANES), BF16),
            jax.ShapeDtypeStruct((b, npair, t + pad_blocks * tm, LANES), BF16),
            jax.ShapeDtypeStruct((b, npair, t + pad_blocks * tm, LANES), BF16),
            jax.ShapeDtypeStruct((b, tm, d), F32),
            jax.ShapeDtypeStruct((b, tm, d), F32),
        ],
        scratch_shapes=[pltpu.VMEM((d, 3 * d), BF16)],
        compiler_params=_params("arbitrary", "arbitrary"),
        name="qkv_proj",
    )(x, g, w)


def _attn_kernel(q_ref, kp_ref, kc_ref, vp_ref, vc_ref, bias_ref, o_ref,
                 kbuf, vbuf, s_even, s_odd, p_even, p_odd, l_even, l_odd, *, tq, invalid_rows):
    i = pl.program_id(1)
    npair, tb = q_ref.shape[1], q_ref.shape[2]
    pad = kp_ref.shape[-2]
    nk = tq + pad
    n_items = (tb // tq) * npair
    assert n_items % 2 == 0 and n_items >= 4 and npair & (npair - 1) == 0
    pair_bits = npair.bit_length() - 1
    if kp_ref.ndim == 3:
        for hp in range(npair):
            kbuf[hp, 0:pad, :] = kp_ref[0, :, hp * LANES:(hp + 1) * LANES].astype(BF16)
            vbuf[hp, 0:pad, :] = vp_ref[0, :, hp * LANES:(hp + 1) * LANES].astype(BF16)
    else:
        kbuf[:, 0:pad, :] = kp_ref[0]
        vbuf[:, 0:pad, :] = vp_ref[0]
    kbuf[:, pad:pad + tb, :] = kc_ref[0]
    vbuf[:, pad:pad + tb, :] = vc_ref[0]

    key = lax.broadcasted_iota(I32, (nk, 2 * tq), 0)
    lane = lax.broadcasted_iota(I32, (1, LANES), 1)
    keep_lo = jnp.where(lane < LANES // 2, 1.0, 0.0).astype(BF16)
    keep_hi = jnp.where(lane < LANES // 2, 0.0, 1.0).astype(BF16)
    out_lo = lax.broadcasted_iota(I32, (tq, LANES), 1) < LANES // 2
    s_bufs, p_bufs, l_bufs = (s_even, s_odd), (p_even, p_odd), (l_even, l_odd)

    def where(item):
        hp = jnp.bitwise_and(item, npair - 1)
        r0 = pl.multiple_of(jnp.right_shift(item, pair_bits) * tq, tq)
        return hp, r0

    def scores(item, slot):
        hp, r0 = where(item)
        q2 = q_ref[0, hp, pl.ds(r0, tq), :]
        qm = jnp.concatenate([q2 * keep_lo, q2 * keep_hi], axis=0)
        s_bufs[slot][...] = lax.dot_general(kbuf[hp, pl.ds(r0, nk), :], qm, (((1,), (1,)), ((), ())),
                                            preferred_element_type=F32)

    def softmax(item, slot, mask_positions):
        hp, r0 = where(item)
        sc = s_bufs[slot][...] + bias_ref[hp]
        if mask_positions:
            sc = jnp.where(key >= invalid_rows - i * tb - r0, sc, MASK_VALUE)
        p = jnp.exp2(sc - jnp.max(sc, axis=0, keepdims=True))
        l_bufs[slot][...] = jnp.sum(p, axis=0, keepdims=True)
        p_bufs[slot][...] = p.astype(BF16)

    def weighted_values(item, slot):
        hp, r0 = where(item)
        o_t = lax.dot_general(vbuf[hp, pl.ds(r0, nk), :], p_bufs[slot][...], (((0,), (0,)), ((), ())),
                              preferred_element_type=F32) / l_bufs[slot][...]
        o = o_t.T
        o_ref[0, hp, pl.ds(r0, tq), :] = jnp.where(out_lo, o[0:tq], o[tq:2 * tq]).astype(o_ref.dtype)

    def run(mask_positions):
        scores(0, 0)
        scores(1, 1)
        softmax(0, 0, mask_positions)

        def two_items(g, carry):
            t = 2 * g
            scores(t, 0)
            softmax(t - 1, 1, mask_positions)
            weighted_values(t - 2, 0)
            scores(t + 1, 1)
            softmax(t, 0, mask_positions)
            weighted_values(t - 1, 1)
            return carry

        lax.fori_loop(1, n_items // 2, two_items, 0)
        softmax(n_items - 1, 1, mask_positions)
        weighted_values(n_items - 2, 0)
        weighted_values(n_items - 1, 1)

    if invalid_rows == 0:
        run(False)
    else:
        touches_invalid = i * tb < invalid_rows
        pl.when(touches_invalid)(lambda: run(True))
        pl.when(jnp.logical_not(touches_invalid))(lambda: run(False))


def _attention(q, k_past, k_cur, v_past, v_cur, bias, *, tb, tq, invalid_rows, cur0):
    b, npair, t, _ = q.shape
    pad = BAND_PAST
    nk = tq + pad
    if k_past.ndim == 3:
        past_spec = pl.BlockSpec((1, pad, k_past.shape[2]), lambda bi, i: (bi, (i * tb) // pad, 0))
    else:
        past_spec = pl.BlockSpec((1, npair, pad, LANES), lambda bi, i: (bi, 0, (i * tb) // pad, 0))
    cur_spec = pl.BlockSpec((1, npair, tb, LANES), lambda bi, i: (bi, 0, cur0 + i, 0))
    return pl.pallas_call(
        functools.partial(_attn_kernel, tq=tq, invalid_rows=invalid_rows),
        grid=(b, t // tb),
        in_specs=[
            pl.BlockSpec((1, npair, tb, LANES), lambda bi, i: (bi, 0, i, 0)),
            past_spec, cur_spec, past_spec, cur_spec,
            pl.BlockSpec(bias.shape, lambda bi, i: (0, 0, 0)),
        ],
        out_specs=pl.BlockSpec((1, npair, tb, LANES), lambda bi, i: (bi, 0, i, 0)),
        out_shape=jax.ShapeDtypeStruct(q.shape, BF16),
        scratch_shapes=[pltpu.VMEM((npair, pad + tb, LANES), BF16), pltpu.VMEM((npair, pad + tb, LANES), BF16),
                        pltpu.VMEM((nk, 2 * tq), F32), pltpu.VMEM((nk, 2 * tq), F32),
                        pltpu.VMEM((nk, 2 * tq), BF16), pltpu.VMEM((nk, 2 * tq), BF16),
                        pltpu.VMEM((1, 2 * tq), F32), pltpu.VMEM((1, 2 * tq), F32)],
        compiler_params=_params("arbitrary", "arbitrary"),
        name="band_attention",
    )(q, k_past, k_cur, v_past, v_cur, bias)


def _band_bias(rel_table, tq):
    nk = tq + BAND_PAST
    dist = np.arange(-(tq - 1), nk)
    per_dist = LOG2_E * rel_table.astype(F32)[:, np.clip(BAND_PAST - dist, -REL_CLIP, REL_CLIP) + REL_CLIP]
    span = per_dist.shape[1]
    skew = jnp.tile(per_dist, (1, tq))[:, tq - 1:tq - 1 + tq * (span - 1)].reshape(-1, tq, span - 1)
    bias = skew[:, :, :nk]
    qc, kc = np.arange(tq)[:, None] // CHUNK, np.arange(nk)[None, :] // CHUNK
    in_band = (kc >= qc) & (kc <= qc + LEFT_CHUNKS)
    bias = jnp.where(in_band[None], bias, MASK_VALUE)
    return bias.reshape(N_HEADS // 2, 2 * tq, nk).transpose(0, 2, 1)


def _router_logits(xn, wcat_ref, br_ref):
    xh = xn.astype(BF16)
    xl = (xn - xh.astype(F32)).astype(BF16)
    both = jnp.dot(xh, wcat_ref[...], preferred_element_type=F32)
    low = jnp.dot(xl, wcat_ref[:, 0:N_ROUTER], preferred_element_type=F32)
    return both[:, 0:N_ROUTER] + both[:, N_ROUTER:2 * N_ROUTER] + low + br_ref[...]


def _ffn_prologue(x1, gf_ref, wcat_ref, br_ref, tri_ref, cnt0_ref, first_step,
                  x1_ref, xn_ref, e1_ref, e2_ref, g1_ref, g2_ref, r1_ref, r2_ref, cnt_ref, carry):
    xn = _rms(x1, gf_ref[...])
    x1_ref[...] = x1
    _store_rows(xn_ref, _pack_bf16_pairs(xn))

    logits = _router_logits(xn, wcat_ref, br_ref)
    lane = lax.broadcasted_iota(I32, logits.shape, 1).astype(F32)
    is_group = lane < N_GROUPS
    neg_inf = -jnp.inf
    far = float(N_ROUTER)
    gl = jnp.where(is_group, logits, neg_inf)
    gmax = jnp.max(gl, axis=-1, keepdims=True)
    g_w = 1.0 / jnp.sum(jnp.exp(gl - gmax), axis=-1, keepdims=True)
    g_idx = jnp.min(jnp.where(gl == gmax, lane, far), axis=-1, keepdims=True)
    eidx = lane - N_GROUPS
    egroup = jnp.floor(eidx * (1.0 / EXPERTS_PER_GROUP))
    el = jnp.where(egroup == g_idx, logits, neg_inf)
    m1 = jnp.max(el, axis=-1, keepdims=True)
    i1 = jnp.min(jnp.where(el == m1, eidx, far), axis=-1, keepdims=True)
    el2 = jnp.where(eidx == i1, neg_inf, el)
    m2 = jnp.max(el2, axis=-1, keepdims=True)
    i2 = jnp.min(jnp.where(el2 == m2, eidx, far), axis=-1, keepdims=True)
    t = jnp.exp(m2 - m1)
    w_a = 1.0 / (1.0 + t)
    e1_ref[...] = i1.astype(I32)
    e2_ref[...] = i2.astype(I32)
    g1_ref[...] = g_w * w_a
    g2_ref[...] = g_w * (t * w_a)

    @pl.when(first_step)
    def _():
        carry[...] = cnt0_ref[...]

    sel1 = eidx == i1
    sel2 = eidx == i2
    onehot = jnp.where(sel1, 1.0, jnp.where(sel2, 1.0, 0.0))
    before = jnp.dot(tri_ref[...], onehot.astype(BF16), preferred_element_type=F32) + carry[...]
    r1_ref[...] = jnp.sum(jnp.where(sel1, before, 0.0), axis=-1, keepdims=True).astype(I32)
    r2_ref[...] = jnp.sum(jnp.where(sel2, before, 0.0), axis=-1, keepdims=True).astype(I32)
    carry[...] = carry[...] + jnp.sum(onehot, axis=0, keepdims=True)
    cnt_ref[...] = carry[...]


def _ffn_out(n, d, tm, row):
    nbp = d // (2 * LANES)
    col = pl.BlockSpec((tm, 1), lambda *g: (row(*g), 0))
    specs = [pl.BlockSpec((tm, d), lambda *g: (row(*g), 0)),
             pl.BlockSpec((tm * nbp, LANES), lambda *g: (row(*g), 0))] + [col] * 6 \
        + [pl.BlockSpec((1, N_ROUTER), lambda *g: (0, 0))]
    shapes = [jax.ShapeDtypeStruct((n, d), F32), jax.ShapeDtypeStruct((n * nbp, LANES), jnp.uint32),
              jax.ShapeDtypeStruct((n, 1), I32), jax.ShapeDtypeStruct((n, 1), I32),
              jax.ShapeDtypeStruct((n, 1), F32), jax.ShapeDtypeStruct((n, 1), F32),
              jax.ShapeDtypeStruct((n, 1), I32), jax.ShapeDtypeStruct((n, 1), I32),
              jax.ShapeDtypeStruct((1, N_ROUTER), F32)]
    return specs, shapes


def _strict_lower(tm):
    r = jnp.arange(tm)
    return (r[:, None] > r[None, :]).astype(BF16)


def _attn_out_kernel(a_ref, x_ref, wo32_ref, gf_ref, wcat_ref, br_ref, tri_ref, cnt0_ref, *rest):
    outs, carry, wo_ref = rest[:-2], rest[-2], rest[-1]

    @pl.when(pl.program_id(0) == 0)
    def _():
        wo_ref[...] = wo32_ref[...].astype(BF16)

    nbatch, npair, rows, _ = a_ref.shape
    a = jnp.concatenate([a_ref[:, hp].reshape(nbatch * rows, LANES) for hp in range(npair)], axis=-1)
    x1 = x_ref[...] + jnp.dot(a, wo_ref[...], preferred_element_type=F32)
    _ffn_prologue(x1, gf_ref, wcat_ref, br_ref, tri_ref, cnt0_ref, pl.program_id(0) == 0, *outs, carry)


def _attn_out(a, x, wo, gf, wcat, br, cnt0, *, tm):
    n, d = x.shape
    b, npair, t, _ = a.shape
    row = lambda i: (i, 0)
    fixed = lambda i: (0, 0)
    tri = _strict_lower(tm)
    specs, shapes = _ffn_out(n, d, tm, lambda i: i)
    if t >= tm:
        a_spec = pl.BlockSpec((1, npair, tm, LANES), lambda i: (i // (t // tm), 0, i % (t // tm), 0))
    else:
        a_spec = pl.BlockSpec((tm // t, npair, t, LANES), lambda i: (i, 0, 0, 0))
    return pl.pallas_call(
        _attn_out_kernel,
        grid=(n // tm,),
        in_specs=[
            a_spec, pl.BlockSpec((tm, d), row),
            pl.BlockSpec(wo.shape, fixed, pipeline_mode=pl.Buffered(1)), pl.BlockSpec(gf.shape, fixed),
            pl.BlockSpec(wcat.shape, fixed), pl.BlockSpec(br.shape, fixed), pl.BlockSpec(tri.shape, fixed),
            pl.BlockSpec(cnt0.shape, fixed),
        ],
        out_specs=specs,
        out_shape=shapes,
        scratch_shapes=[pltpu.VMEM((1, N_ROUTER), F32), pltpu.VMEM(wo.shape, BF16)],
        compiler_params=_params("arbitrary"),
        name="attn_out_router",
    )(a, x, wo, gf, wcat, br, tri, cnt0)


def _conv_kernel(x_ref, st_ref, gm_ref, w1_ref, b1_ref, wdw_ref, bdw_ref, lg_ref, lb_ref, w2_ref, b2_ref,
                 gf_ref, wcat_ref, br_ref, tri_ref, cnt0_ref, *rest, width):
    outs, ns_ref = rest[:9], rest[9]
    ubuf, rbuf, dwbuf, carry = rest[10:]
    bi = pl.program_id(0)
    t = pl.program_id(1)
    tt = x_ref.shape[1]
    d = x_ref.shape[2]
    nb = d // LANES
    state = width - 1
    lo = CONV_HEAD - state
    shifted_rows = rbuf.shape[1]
    rb = min(CONV_ROW_BLOCK, tt)

    @pl.when(t == 0)
    def _():
        for c in range(nb):
            ubuf[c, lo:CONV_HEAD, :] = st_ref[0, :, c * LANES:(c + 1) * LANES]

    x = x_ref[0]
    xn = _rms(x, gm_ref[...]).astype(BF16)

    def glu(p):
        hid = jnp.dot(xn, w1_ref[p], preferred_element_type=F32) + b1_ref[p]
        for k in range(2):
            u = hid[:, k * LANES:(k + 1) * LANES] * jax.nn.sigmoid(hid[:, (2 + k) * LANES:(3 + k) * LANES])
            ubuf[2 * p + k, CONV_HEAD:CONV_HEAD + tt, :] = u
            ns_ref[0, 2 * p + k] = u[tt - state:tt, :]

    def depthwise(c):
        for b in range(1, SUBLANES):
            rbuf[b - 1] = ubuf[c, b:b + shifted_rows, :]
        for r in range(tt // rb):
            r0 = r * rb
            acc = jnp.broadcast_to(bdw_ref[c], (rb, LANES))
            for j in range(width):
                a8, b = divmod(lo + j, SUBLANES)
                if b == 0:
                    src = ubuf[c, lo + j + r0:lo + j + r0 + rb, :]
                else:
                    src = rbuf[b - 1, a8 * SUBLANES + r0:a8 * SUBLANES + r0 + rb, :]
                acc = acc + src * wdw_ref[c, j:j + 1, :]
            dwbuf[c, r0:r0 + rb, :] = acc
        ubuf[c, lo:CONV_HEAD, :] = ubuf[c, lo + tt:CONV_HEAD + tt, :]

    glu(0)
    for p in range(nb // 2):
        if p + 1 < nb // 2:
            glu(p + 1)
        depthwise(2 * p)
        depthwise(2 * p + 1)

    dw = jnp.concatenate([dwbuf[c] for c in range(nb)], axis=-1)
    mu = jnp.mean(dw, axis=-1, keepdims=True)
    xc = dw - mu
    z = xc * lax.rsqrt(jnp.mean(xc * xc, axis=-1, keepdims=True) + LN_EPS) * lg_ref[...] + lb_ref[...]
    z = z * jax.nn.sigmoid(z)
    x1 = x + jnp.dot(z.astype(BF16), w2_ref[...], preferred_element_type=F32) + b2_ref[...]
    first = jnp.logical_and(bi == 0, t == 0)
    _ffn_prologue(x1, gf_ref, wcat_ref, br_ref, tri_ref, cnt0_ref, first, *outs, carry)


def _conv_layer(x, state, gm, w1, b1, wdw, bdw, lg, lb, w2, b2, gf, wcat, br, cnt0, *, tt):
    b, t, d = x.shape
    width = wdw.shape[0]
    nb = d // LANES
    nt = t // tt
    wdw3 = wdw.reshape(width, nb, LANES).transpose(1, 0, 2)
    bdw3 = bdw.reshape(nb, 1, LANES)
    pair_cols = lambda m: jnp.concatenate([m[..., :d].reshape(-1, nb // 2, 2 * LANES),
                                           m[..., d:].reshape(-1, nb // 2, 2 * LANES)], axis=2).transpose(1, 0, 2)
    w1 = pair_cols(w1)
    b1 = pair_cols(b1)
    tri = _strict_lower(tt)
    fixed2 = lambda bi, ti: (0, 0)
    fixed3 = lambda bi, ti: (0, 0, 0)
    specs, shapes = _ffn_out(b * t, d, tt, lambda bi, ti: bi * nt + ti)
    shifted_rows = tt + ((CONV_HEAD - 1) // SUBLANES) * SUBLANES
    return pl.pallas_call(
        functools.partial(_conv_kernel, width=width),
        grid=(b, nt),
        in_specs=[
            pl.BlockSpec((1, tt, d), lambda bi, ti: (bi, ti, 0)),
            pl.BlockSpec((1, width - 1, d), lambda bi, ti: (bi, 0, 0)),
            pl.BlockSpec(gm.shape, fixed2), pl.BlockSpec(w1.shape, fixed3), pl.BlockSpec(b1.shape, fixed3),
            pl.BlockSpec(wdw3.shape, fixed3), pl.BlockSpec(bdw3.shape, fixed3),
            pl.BlockSpec(lg.shape, fixed2), pl.BlockSpec(lb.shape, fixed2),
            pl.BlockSpec(w2.shape, fixed2), pl.BlockSpec(b2.shape, fixed2),
            pl.BlockSpec(gf.shape, fixed2), pl.BlockSpec(wcat.shape, fixed2), pl.BlockSpec(br.shape, fixed2),
            pl.BlockSpec(tri.shape, fixed2), pl.BlockSpec(cnt0.shape, fixed2),
        ],
        out_specs=specs + [pl.BlockSpec((1, nb, width - 1, LANES), lambda bi, ti: (bi, 0, 0, 0))],
        out_shape=shapes + [jax.ShapeDtypeStruct((b, nb, width - 1, LANES), F32)],
        scratch_shapes=[
            pltpu.VMEM((nb, CONV_HEAD + tt, LANES), F32),
            pltpu.VMEM((SUBLANES - 1, shifted_rows, LANES), F32),
            pltpu.VMEM((nb, tt, LANES), F32),
            pltpu.VMEM((1, N_ROUTER), F32),
        ],
        compiler_params=_params("arbitrary", "arbitrary"),
        name="conv_module_router",
    )(x, state, gm, w1, b1, wdw3, bdw3, lg, lb, w2, b2, gf, wcat, br, tri, cnt0)


def _slot(starts_ref, meta_ref, tm, which, r):
    e = meta_ref[0, 0, which * tm + r]
    return starts_ref[e] + meta_ref[0, 0, (2 + which) * tm + r]


def _row_copy(src, src_row, dst, dst_row, sem, nb):
    return pltpu.make_async_copy(src.at[pl.ds(pl.multiple_of(src_row * nb, nb), nb), :],
                                 dst.at[pl.ds(pl.multiple_of(dst_row * nb, nb), nb), :], sem)


def _dispatch_kernel(starts_ref, ends_ref, meta_ref, *rest, tm, moe_tm, nb, group_steps):
    n_groups = len(group_steps)
    xn_refs = rest[:n_groups]
    xs_hbm, zbuf, sem, zsem = rest[n_groups:]
    i = pl.program_id(0)

    @pl.when(i == 0)
    def _():
        zbuf[...] = jnp.zeros_like(zbuf)

        def tile_copy(e):
            return pltpu.make_async_copy(
                zbuf, xs_hbm.at[pl.ds(pl.multiple_of((ends_ref[e] - moe_tm) * nb, nb), moe_tm * nb), :], zsem)

        for e in range(N_EXPERTS):
            @pl.when(ends_ref[e] > starts_ref[e])
            def _():
                tile_copy(e).start()
        for e in range(N_EXPERTS):
            @pl.when(ends_ref[e] > starts_ref[e])
            def _():
                tile_copy(e).wait()

        def unused_tile(j):
            return pltpu.make_async_copy(
                zbuf, xs_hbm.at[pl.ds(pl.multiple_of(j * (moe_tm * nb), moe_tm * nb), moe_tm * nb), :], zsem)

        first_unused = ends_ref[N_EXPERTS - 1] // moe_tm
        n_tiles = xs_hbm.shape[0] // (moe_tm * nb)
        lax.fori_loop(first_unused, n_tiles, lambda j, c: (unused_tile(j).start(), c)[1], 0)
        lax.fori_loop(first_unused, n_tiles, lambda j, c: (unused_tile(j).wait(), c)[1], 0)

    def scatter_tile(xn_ref):
        def copies(r):
            return [_row_copy(xn_ref, r, xs_hbm, _slot(starts_ref, meta_ref, tm, w, r), sem, nb) for w in range(2)]

        def issue(g, carry):
            for k in range(DMA_UNROLL):
                for queue, cp in enumerate(copies(g * DMA_UNROLL + k)):
                    cp.start(priority=queue)
            return carry

        def drain(g, carry):
            for k in range(DMA_UNROLL):
                for cp in copies(g * DMA_UNROLL + k):
                    cp.wait()
            return carry

        lax.fori_loop(0, tm // DMA_UNROLL, issue, 0)
        lax.fori_loop(0, tm // DMA_UNROLL, drain, 0)

    first = 0
    for xn_ref, steps in zip(xn_refs, group_steps):
        pl.when(jnp.logical_and(i >= first, i < first + steps))(functools.partial(scatter_tile, xn_ref))
        first += steps


def _dispatch(starts, ends, meta, xn_rows_list, *, d, tm, moe_tm, p_max):
    nb = d // (2 * LANES)
    group_steps = tuple(x.shape[0] // (nb * tm) for x in xn_rows_list)
    firsts = tuple(sum(group_steps[:k]) for k in range(len(group_steps)))

    def tile_of(k):
        return lambda i, s, e: (jnp.clip(i - firsts[k], 0, group_steps[k] - 1), 0)

    return pl.pallas_call(
        functools.partial(_dispatch_kernel, tm=tm, moe_tm=moe_tm, nb=nb, group_steps=group_steps),
        grid_spec=pltpu.PrefetchScalarGridSpec(
            num_scalar_prefetch=2,
            grid=(sum(group_steps),),
            in_specs=[pl.BlockSpec((1, 1, 4 * tm), lambda i, s, e: (i, 0, 0), memory_space=pltpu.SMEM)]
            + [pl.BlockSpec((tm * nb, LANES), tile_of(k)) for k in range(len(group_steps))],
            out_specs=pl.BlockSpec(memory_space=pl.ANY),
            scratch_shapes=[pltpu.VMEM((moe_tm * nb, LANES), jnp.uint32),
                            pltpu.SemaphoreType.DMA, pltpu.SemaphoreType.DMA],
        ),
        out_shape=jax.ShapeDtypeStruct((p_max * nb, LANES), jnp.uint32),
        compiler_params=_params("arbitrary"),
        name="moe_dispatch",
    )(starts, ends, meta, *xn_rows_list)


def _moe_kernel(te_ref, nv_ref, x_ref, wg_ref, wu_ref, wd_ref, o_ref, wgu, wdn, *, tm, nb):
    i = pl.program_id(0)
    f = wg_ref.shape[3]

    @pl.when(jnp.logical_or(i == 0, te_ref[i] != te_ref[jnp.maximum(i - 1, 0)]))
    def _():
        wgu[:, 0:f] = wg_ref[0, 0].astype(BF16)
        wgu[:, f:2 * f] = wu_ref[0, 0].astype(BF16)
        wdn[...] = wd_ref[0, 0].astype(BF16)

    @pl.when(i < nv_ref[0])
    def _():
        x = _unpack_bf16_pairs(_load_rows(x_ref, tm, nb // 2))
        hu = jnp.dot(x, wgu[...], preferred_element_type=F32)
        h, u = hu[:, 0:f], hu[:, f:2 * f]
        hid = (h * jax.nn.sigmoid(h)) * u
        _store_rows(o_ref, jnp.dot(hid.astype(BF16), wdn[...], preferred_element_type=F32))

    @pl.when(i >= nv_ref[0])
    def _():
        o_ref[...] = jnp.zeros_like(o_ref)


def _moe_grouped(tile_expert, n_valid, xs_rows, wg, wu, wd, *, layer, tm, d):
    nb = d // LANES
    p = xs_rows.shape[0] // (nb // 2)
    f = wg.shape[3]
    tile = lambda i, te, nv: (jnp.minimum(i, nv[0] - 1), 0)
    wmap = lambda i, te, nv: (layer, te[i], 0, 0)
    return pl.pallas_call(
        functools.partial(_moe_kernel, tm=tm, nb=nb),
        grid_spec=pltpu.PrefetchScalarGridSpec(
            num_scalar_prefetch=2,
            grid=(p // tm,),
            in_specs=[
                pl.BlockSpec((tm * nb // 2, LANES), tile),
                pl.BlockSpec((1, 1, d, f), wmap), pl.BlockSpec((1, 1, d, f), wmap),
                pl.BlockSpec((1, 1, f, d), wmap),
            ],
            out_specs=pl.BlockSpec((tm * nb, LANES), lambda i, te, nv: (i, 0)),
            scratch_shapes=[pltpu.VMEM((d, 2 * f), BF16), pltpu.VMEM((f, d), BF16)],
        ),
        out_shape=jax.ShapeDtypeStruct((p * nb, LANES), F32),
        compiler_params=_params("arbitrary"),
        name="moe_grouped_mlp",
    )(tile_expert, n_valid, xs_rows, wg, wu, wd)


def _combine_kernel(starts_ref, meta_ref, meta_next_ref, x1_ref, g1_ref, g2_ref, gfin_ref, ys_hbm, o_ref,
                    buf, sem, *, tm, nb, final_norm):
    i = pl.program_id(0)
    n_steps = pl.num_programs(0)

    def copies(meta, slot, r):
        return [_row_copy(ys_hbm, _slot(starts_ref, meta, tm, w, r), buf.at[slot, w], r, sem.at[slot], nb)
                for w in range(2)]

    def issue(meta, slot):
        def body(g, carry):
            for k in range(DMA_UNROLL):
                for queue, cp in enumerate(copies(meta, slot, g * DMA_UNROLL + k)):
                    cp.start(priority=queue)
            return carry
        lax.fori_loop(0, tm // DMA_UNROLL, body, 0)

    def drain(meta, slot):
        def body(g, carry):
            for k in range(DMA_UNROLL):
                for cp in copies(meta, slot, g * DMA_UNROLL + k):
                    cp.wait()
            return carry
        lax.fori_loop(0, tm // DMA_UNROLL, body, 0)

    slot = jnp.bitwise_and(i, 1)

    @pl.when(i == 0)
    def _():
        issue(meta_ref, 0)

    @pl.when(i + 1 < n_steps)
    def _():
        issue(meta_next_ref, 1 - slot)

    drain(meta_ref, slot)
    y = (x1_ref[...] + g1_ref[...] * _load_rows(buf.at[slot, 0], tm, nb)
         + g2_ref[...] * _load_rows(buf.at[slot, 1], tm, nb))
    o_ref[...] = _rms(y, gfin_ref[...]) if final_norm else y


def _combine(starts, meta, x1, g1, g2, gfin, ys_rows, *, tm, final_norm):
    n, d = x1.shape
    nb = d // LANES
    n_steps = n // tm
    row = lambda i, s: (i, 0)
    return pl.pallas_call(
        functools.partial(_combine_kernel, tm=tm, nb=nb, final_norm=final_norm),
        grid_spec=pltpu.PrefetchScalarGridSpec(
            num_scalar_prefetch=1,
            grid=(n_steps,),
            in_specs=[
                pl.BlockSpec((1, 1, 4 * tm), lambda i, s: (i, 0, 0), memory_space=pltpu.SMEM),
                pl.BlockSpec((1, 1, 4 * tm), lambda i, s: (jnp.minimum(i + 1, n_steps - 1), 0, 0),
                             memory_space=pltpu.SMEM),
                pl.BlockSpec((tm, d), row), pl.BlockSpec((tm, 1), row), pl.BlockSpec((tm, 1), row),
                pl.BlockSpec((1, d), lambda i, s: (0, 0)),
                pl.BlockSpec(memory_space=pl.ANY),
            ],
            out_specs=pl.BlockSpec((tm, d), row),
            scratch_shapes=[pltpu.VMEM((2, 2, tm * nb, LANES), F32), pltpu.SemaphoreType.DMA((2,))],
        ),
        out_shape=jax.ShapeDtypeStruct((n, d), F32),
        compiler_params=_params("arbitrary"),
        name="moe_combine",
    )(starts, meta, meta, x1, g1, g2, gfin, ys_rows)


def _moe(ffns, wg, wu, wd, gfin, *, layer, final_norm):
    d = ffns[0][0].shape[1]
    moe_tm = MOE_ROW_TILE
    n_all = sum(f[0].shape[0] for f in ffns)
    counts = ffns[-1][8][0, N_GROUPS:].astype(I32)
    padded = ((counts + moe_tm - 1) // moe_tm) * moe_tm
    ends = jnp.cumsum(padded)
    starts = ends - padded
    p_max = 2 * n_all + N_EXPERTS * moe_tm
    n_tiles = p_max // moe_tm
    n_valid = ends[-1] // moe_tm
    tile_start = jnp.arange(n_tiles, dtype=I32) * moe_tm
    tile_expert = jnp.sum((tile_start[:, None] >= ends[None, :]).astype(I32), axis=1)
    last_expert = jnp.sum((ends[-1] - 1 >= ends).astype(I32))
    tile_expert = jnp.minimum(tile_expert, last_expert)
    tm = math.gcd(MOE_IO_TILE, *[f[0].shape[0] for f in ffns])
    metas = [jnp.concatenate([v.reshape(-1, tm) for v in (f[2], f[3], f[6], f[7])], axis=1).reshape(-1, 1, 4 * tm)
             for f in ffns]
    xs_rows = _dispatch(starts, ends, jnp.concatenate(metas, axis=0), [f[1] for f in ffns],
                        d=d, tm=tm, moe_tm=moe_tm, p_max=p_max)
    ys_rows = _moe_grouped(tile_expert, n_valid[None], xs_rows, wg, wu, wd, layer=layer, tm=moe_tm, d=d)
    return [_combine(starts, meta, f[0], f[4], f[5], gfin, ys_rows, tm=tm, final_norm=final_norm)
            for f, meta in zip(ffns, metas)]


def kernel(x_prompt, x_sample, cache_attn_k, cache_attn_v, cache_conv, norm_mix, norm_ffn, norm_final,
           w_qkv, w_o, rel_table, w_pw1, b_pw1, w_dw, b_dw, ln_g, ln_b, w_pw2, b_pw2,
           w_group, b_group, w_router, b_router, w_gate, w_up, w_down):
    bp, tp, d = x_prompt.shape
    bs, ts, _ = x_sample.shape
    depth = norm_mix.shape[0]
    kv_keep = cache_attn_k.shape[2]
    conv_state = cache_conv.shape[2]
    assert kv_keep == BAND_PAST and tp % ROW_TILE == 0 and ts == CHUNK and ROW_TILE == BAND_PAST
    assert w_group.shape[2] == N_GROUPS and w_router.shape[2] == N_EXPERTS
    tm_s = min(ROW_TILE, bs * ts)

    groups = ((bp, tp, ROW_TILE), (bs, ts, tm_s))
    xs_cur = [x_prompt.reshape(bp * tp, d), x_sample.reshape(bs * ts, d)]
    kv_out = [[[], []], [[], []]]
    conv_out = [[], []]
    gfin = norm_final[None, :]

    for layer in range(depth):
        gm = norm_mix[layer][None, :]
        gf = norm_ffn[layer][None, :]
        wr = jnp.concatenate([w_group[layer], w_router[layer]], axis=1)
        wr_hi = wr.astype(BF16)
        wcat = jnp.concatenate([wr_hi, (wr - wr_hi.astype(F32)).astype(BF16)], axis=1)
        br = jnp.concatenate([b_group[layer], b_router[layer]])[None, :]
        ffn = []
        cnt0 = jnp.zeros((1, N_ROUTER), F32)
        if layer % 2 == 0:
            a = layer // 2
            wqkv = w_qkv[a]
            wo = w_o[a]
            for gi, (b, t, tm) in enumerate(groups):
                x3 = xs_cur[gi].reshape(b, t, d)
                if gi == 0:
                    q, kpad, vpad, k32, v32 = _qkv(x3, gm, wqkv, tm=ROW_TILE, pad_blocks=1)
                    bias = _band_bias(rel_table[a], ATTN_Q_ROWS)
                    att = _attention(q, kpad, kpad, vpad, vpad, bias, tb=ROW_TILE, tq=ATTN_Q_ROWS,
                                     invalid_rows=BAND_PAST, cur0=BAND_PAST // ROW_TILE)
                    k_new, v_new = k32, v32
                else:
                    q, kn, vn, k32, v32 = _qkv(x3, gm, wqkv, tm=t, pad_blocks=0)
                    ck = cache_attn_k[a].reshape(b, kv_keep, d)
                    cv = cache_attn_v[a].reshape(b, kv_keep, d)
                    bias = _band_bias(rel_table[a], t)
                    att = _attention(q, ck, kn, cv, vn, bias, tb=t, tq=t, invalid_rows=0, cur0=0)
                    k_new = jnp.concatenate([ck, k32], axis=1)[:, -kv_keep:]
                    v_new = jnp.concatenate([cv, v32], axis=1)[:, -kv_keep:]
                kv_out[gi][0].append(k_new.reshape(b, kv_keep, N_HEADS, d // N_HEADS))
                kv_out[gi][1].append(v_new.reshape(b, kv_keep, N_HEADS, d // N_HEADS))
                ffn.append(_attn_out(att, xs_cur[gi], wo, gf, wcat, br, cnt0, tm=tm))
                cnt0 = ffn[-1][8]
        else:
            c = layer // 2
            w1 = w_pw1[c].astype(BF16)
            w2 = w_pw2[c].astype(BF16)
            for gi, (b, t, tm) in enumerate(groups):
                x3 = xs_cur[gi].reshape(b, t, d)
                state = jnp.zeros((b, conv_state, d), F32) if gi == 0 else cache_conv[c]
                outs = _conv_layer(x3, state, gm, w1, b_pw1[c][None, :], w_dw[c], b_dw[c], ln_g[c][None, :],
                                   ln_b[c][None, :], w2, b_pw2[c][None, :], gf, wcat, br, cnt0,
                                   tt=ROW_TILE if gi == 0 else t)
                ffn.append(outs[:9])
                cnt0 = outs[8]
                conv_out[gi].append(outs[9].transpose(0, 2, 1, 3).reshape(b, conv_state, d))
        last = layer == depth - 1
        xs_cur = _moe(ffn, w_gate, w_up, w_down, gfin, layer=layer, final_norm=last)

    y_prompt = xs_cur[0].reshape(bp, tp, d)
    y_sample = xs_cur[1].reshape(bs, ts, d)
    return (y_prompt, y_sample,
            jnp.stack(kv_out[0][0]), jnp.stack(kv_out[0][1]), jnp.stack(conv_out[0]),
            jnp.stack(kv_out[1][0]), jnp.stack(kv_out[1][1]), jnp.stack(conv_out[1]))
```

```python
import functools
import math

import numpy as np
import jax
import jax.numpy as jnp
from jax import lax
from jax.experimental import pallas as pl
from jax.experimental.pallas import tpu as pltpu

F32 = jnp.float32
BF16 = jnp.bfloat16
I32 = jnp.int32

CHUNK = 64
LEFT_CHUNKS = 8
BAND_PAST = LEFT_CHUNKS * CHUNK
N_HEADS = 16
REL_CLIP = 128
N_GROUPS = 4
EXPERTS_PER_GROUP = 8
N_EXPERTS = N_GROUPS * EXPERTS_PER_GROUP
N_ROUTER = N_GROUPS + N_EXPERTS
RMS_EPS = 1e-6
LN_EPS = 1e-5
LOG2_E = 1.4426950408889634

LANES = 128
SUBLANES = 8
VMEM_LIMIT_BYTES = 56 * 1024 * 1024

MASK_VALUE = -1e30
ROW_TILE = 512
ATTN_Q_ROWS = 128
MOE_ROW_TILE = 512
MOE_IO_TILE = 1024
DMA_UNROLL = 32
CONV_HEAD = 32
CONV_ROW_BLOCK = 128


def _params(*sem):
    return pltpu.CompilerParams(dimension_semantics=sem, vmem_limit_bytes=VMEM_LIMIT_BYTES)


def _rms(x, g):
    return x * lax.rsqrt(jnp.mean(x * x, axis=-1, keepdims=True) + RMS_EPS) * g


def _store_rows(ref, val):
    rows, d = val.shape
    nb = d // LANES
    for c in range(nb):
        ref[pl.ds(c, rows, stride=nb), :] = val[:, c * LANES:(c + 1) * LANES]


def _load_rows(ref, rows, nb):
    return jnp.concatenate([ref[pl.ds(c, rows, stride=nb), :] for c in range(nb)], axis=-1)


def _pack_bf16_pairs(x):
    half = x.shape[1] // 2
    bits = pltpu.bitcast(x.astype(BF16).astype(F32), jnp.uint32)
    return jnp.right_shift(bits[:, :half], jnp.uint32(16)) | bits[:, half:]


def _unpack_bf16_pairs(w):
    lo = pltpu.bitcast(jnp.left_shift(w, jnp.uint32(16)), F32)
    hi = pltpu.bitcast(w & jnp.uint32(0xFFFF0000), F32)
    return jnp.concatenate([lo, hi], axis=-1).astype(BF16)


def _qkv_kernel(x_ref, g_ref, w32_ref, q_ref, k_ref, v_ref, k32_ref, v32_ref, w_ref, *, pad_blocks, d):
    j = pl.program_id(1)

    @pl.when(jnp.logical_and(pl.program_id(0) == 0, j == 0))
    def _():
        w_ref[...] = w32_ref[...].astype(BF16)

    @pl.when(j < pad_blocks)
    def _():
        k_ref[...] = jnp.zeros_like(k_ref)
        v_ref[...] = jnp.zeros_like(v_ref)

    @pl.when(j >= pad_blocks)
    def _():
        xn = _rms(x_ref[0], g_ref[...]).astype(BF16)
        q = jnp.dot(xn, w_ref[:, 0:d], preferred_element_type=F32) * (LOG2_E * float(d // N_HEADS) ** -0.5)
        k = jnp.dot(xn, w_ref[:, d:2 * d], preferred_element_type=F32)
        v = jnp.dot(xn, w_ref[:, 2 * d:3 * d], preferred_element_type=F32)
        k32_ref[0] = k
        v32_ref[0] = v
        for hp in range(d // LANES):
            cols = slice(hp * LANES, (hp + 1) * LANES)
            q_ref[0, hp] = q[:, cols].astype(BF16)
            k_ref[0, hp] = k[:, cols].astype(BF16)
            v_ref[0, hp] = v[:, cols].astype(BF16)


def _qkv(x, g, w, *, tm, pad_blocks):
    b, t, d = x.shape
    nt = t // tm
    npair = d // LANES
    xmap = lambda bi, j: (bi, jnp.maximum(j - pad_blocks, 0), 0)
    return pl.pallas_call(
        functools.partial(_qkv_kernel, pad_blocks=pad_blocks, d=d),
        grid=(b, nt + pad_blocks),
        in_specs=[
            pl.BlockSpec((1, tm, d), xmap),
            pl.BlockSpec((1, d), lambda bi, j: (0, 0)),
            pl.BlockSpec((d, 3 * d), lambda bi, j: (0, 0), pipeline_mode=pl.Buffered(1)),
        ],
        out_specs=[
            pl.BlockSpec((1, npair, tm, LANES), lambda bi, j: (bi, 0, jnp.maximum(j - pad_blocks, 0), 0)),
            pl.BlockSpec((1, npair, tm, LANES), lambda bi, j: (bi, 0, j, 0)),
            pl.BlockSpec((1, npair, tm, LANES), lambda bi, j: (bi, 0, j, 0)),
            pl.BlockSpec((1, tm, d), lambda bi, j: (bi, 0, 0)),
            pl.BlockSpec((1, tm, d), lambda bi, j: (bi, 0, 0)),
        ],
        out_shape=[
            jax.ShapeDtypeStruct((b, npair, t, LANES), BF16),
            jax.ShapeDtypeStruct((b, npair, t + pad_blocks * tm, LANES), BF16),
            jax.ShapeDtypeStruct((b, npair, t + pad_blocks * tm, LANES), BF16),
            jax.ShapeDtypeStruct((b, tm, d), F32),
            jax.ShapeDtypeStruct((b, tm, d), F32),
        ],
        scratch_shapes=[pltpu.VMEM((d, 3 * d), BF16)],
        compiler_params=_params("arbitrary", "arbitrary"),
        name="qkv_proj",
    )(x, g, w)


def _attn_kernel(q_ref, kp_ref, kc_ref, vp_ref, vc_ref, bias_ref, o_ref,
                 kbuf, vbuf, s_even, s_odd, p_even, p_odd, l_even, l_odd, *, tq, invalid_rows):
    i = pl.program_id(1)
    npair, tb = q_ref.shape[1], q_ref.shape[2]
    pad = kp_ref.shape[-2]
    nk = tq + pad
    n_items = (tb // tq) * npair
    assert n_items % 2 == 0 and n_items >= 4 and npair & (npair - 1) == 0
    pair_bits = npair.bit_length() - 1
    if kp_ref.ndim == 3:
        for hp in range(npair):
            kbuf[hp, 0:pad, :] = kp_ref[0, :, hp * LANES:(hp + 1) * LANES].astype(BF16)
            vbuf[hp, 0:pad, :] = vp_ref[0, :, hp * LANES:(hp + 1) * LANES].astype(BF16)
    else:
        kbuf[:, 0:pad, :] = kp_ref[0]
        vbuf[:, 0:pad, :] = vp_ref[0]
    kbuf[:, pad:pad + tb, :] = kc_ref[0]
    vbuf[:, pad:pad + tb, :] = vc_ref[0]

    key = lax.broadcasted_iota(I32, (nk, 2 * tq), 0)
    lane = lax.broadcasted_iota(I32, (1, LANES), 1)
    keep_lo = jnp.where(lane < LANES // 2, 1.0, 0.0).astype(BF16)
    keep_hi = jnp.where(lane < LANES // 2, 0.0, 1.0).astype(BF16)
    out_lo = lax.broadcasted_iota(I32, (tq, LANES), 1) < LANES // 2
    s_bufs, p_bufs, l_bufs = (s_even, s_odd), (p_even, p_odd), (l_even, l_odd)

    def where(item):
        hp = jnp.bitwise_and(item, npair - 1)
        r0 = pl.multiple_of(jnp.right_shift(item, pair_bits) * tq, tq)
        return hp, r0

    def scores(item, slot):
        hp, r0 = where(item)
        q2 = q_ref[0, hp, pl.ds(r0, tq), :]
        qm = jnp.concatenate([q2 * keep_lo, q2 * keep_hi], axis=0)
        s_bufs[slot][...] = lax.dot_general(kbuf[hp, pl.ds(r0, nk), :], qm, (((1,), (1,)), ((), ())),
                                            preferred_element_type=F32)

    def softmax(item, slot, mask_positions):
        hp, r0 = where(item)
        sc = s_bufs[slot][...] + bias_ref[hp]
        if mask_positions:
            sc = jnp.where(key >= invalid_rows - i * tb - r0, sc, MASK_VALUE)
        p = jnp.exp2(sc - jnp.max(sc, axis=0, keepdims=True))
        l_bufs[slot][...] = jnp.sum(p, axis=0, keepdims=True)
        p_bufs[slot][...] = p.astype(BF16)

    def weighted_values(item, slot):
        hp, r0 = where(item)
        o_t = lax.dot_general(vbuf[hp, pl.ds(r0, nk), :], p_bufs[slot][...], (((0,), (0,)), ((), ())),
                              preferred_element_type=F32) / l_bufs[slot][...]
        o = o_t.T
        o_ref[0, hp, pl.ds(r0, tq), :] = jnp.where(out_lo, o[0:tq], o[tq:2 * tq]).astype(o_ref.dtype)

    def run(mask_positions):
        scores(0, 0)
        scores(1, 1)
        softmax(0, 0, mask_positions)

        def two_items(g, carry):
            t = 2 * g
            scores(t, 0)
            softmax(t - 1, 1, mask_positions)
            weighted_values(t - 2, 0)
            scores(t + 1, 1)
            softmax(t, 0, mask_positions)
            weighted_values(t - 1, 1)
            return carry

        lax.fori_loop(1, n_items // 2, two_items, 0)
        softmax(n_items - 1, 1, mask_positions)
        weighted_values(n_items - 2, 0)
        weighted_values(n_items - 1, 1)

    if invalid_rows == 0:
        run(False)
    else:
        touches_invalid = i * tb < invalid_rows
        pl.when(touches_invalid)(lambda: run(True))
        pl.when(jnp.logical_not(touches_invalid))(lambda: run(False))


def _attention(q, k_past, k_cur, v_past, v_cur, bias, *, tb, tq, invalid_rows, cur0):
    b, npair, t, _ = q.shape
    pad = BAND_PAST
    nk = tq + pad
    if k_past.ndim == 3:
        past_spec = pl.BlockSpec((1, pad, k_past.shape[2]), lambda bi, i: (bi, (i * tb) // pad, 0))
    else:
        past_spec = pl.BlockSpec((1, npair, pad, LANES), lambda bi, i: (bi, 0, (i * tb) // pad, 0))
    cur_spec = pl.BlockSpec((1, npair, tb, LANES), lambda bi, i: (bi, 0, cur0 + i, 0))
    return pl.pallas_call(
        functools.partial(_attn_kernel, tq=tq, invalid_rows=invalid_rows),
        grid=(b, t // tb),
        in_specs=[
            pl.BlockSpec((1, npair, tb, LANES), lambda bi, i: (bi, 0, i, 0)),
            past_spec, cur_spec, past_spec, cur_spec,
            pl.BlockSpec(bias.shape, lambda bi, i: (0, 0, 0)),
        ],
        out_specs=pl.BlockSpec((1, npair, tb, LANES), lambda bi, i: (bi, 0, i, 0)),
        out_shape=jax.ShapeDtypeStruct(q.shape, BF16),
        scratch_shapes=[pltpu.VMEM((npair, pad + tb, LANES), BF16), pltpu.VMEM((npair, pad + tb, LANES), BF16),
                        pltpu.VMEM((nk, 2 * tq), F32), pltpu.VMEM((nk, 2 * tq), F32),
                        pltpu.VMEM((nk, 2 * tq), BF16), pltpu.VMEM((nk, 2 * tq), BF16),
                        pltpu.VMEM((1, 2 * tq), F32), pltpu.VMEM((1, 2 * tq), F32)],
        compiler_params=_params("arbitrary", "arbitrary"),
        name="band_attention",
    )(q, k_past, k_cur, v_past, v_cur, bias)


def _band_bias(rel_table, tq):
    nk = tq + BAND_PAST
    dist = np.arange(-(tq - 1), nk)
    per_dist = LOG2_E * rel_table.astype(F32)[:, np.clip(BAND_PAST - dist, -REL_CLIP, REL_CLIP) + REL_CLIP]
    span = per_dist.shape[1]
    skew = jnp.tile(per_dist, (1, tq))[:, tq - 1:tq - 1 + tq * (span - 1)].reshape(-1, tq, span - 1)
    bias = skew[:, :, :nk]
    qc, kc = np.arange(tq)[:, None] // CHUNK, np.arange(nk)[None, :] // CHUNK
    in_band = (kc >= qc) & (kc <= qc + LEFT_CHUNKS)
    bias = jnp.where(in_band[None], bias, MASK_VALUE)
    return bias.reshape(N_HEADS // 2, 2 * tq, nk).transpose(0, 2, 1)


def _router_logits(xn, wcat_ref, br_ref):
    xh = xn.astype(BF16)
    xl = (xn - xh.astype(F32)).astype(BF16)
    both = jnp.dot(xh, wcat_ref[...], preferred_element_type=F32)
    low = jnp.dot(xl, wcat_ref[:, 0:N_ROUTER], preferred_element_type=F32)
    return both[:, 0:N_ROUTER] + both[:, N_ROUTER:2 * N_ROUTER] + low + br_ref[...]


def _ffn_prologue(x1, gf_ref, wcat_ref, br_ref, tri_ref, cnt0_ref, first_step,
                  x1_ref, xn_ref, e1_ref, e2_ref, g1_ref, g2_ref, r1_ref, r2_ref, cnt_ref, carry):
    xn = _rms(x1, gf_ref[...])
    x1_ref[...] = x1
    _store_rows(xn_ref, _pack_bf16_pairs(xn))

    logits = _router_logits(xn, wcat_ref, br_ref)
    lane = lax.broadcasted_iota(I32, logits.shape, 1).astype(F32)
    is_group = lane < N_GROUPS
    neg_inf = -jnp.inf
    far = float(N_ROUTER)
    gl = jnp.where(is_group, logits, neg_inf)
    gmax = jnp.max(gl, axis=-1, keepdims=True)
    g_w = 1.0 / jnp.sum(jnp.exp(gl - gmax), axis=-1, keepdims=True)
    g_idx = jnp.min(jnp.where(gl == gmax, lane, far), axis=-1, keepdims=True)
    eidx = lane - N_GROUPS
    egroup = jnp.floor(eidx * (1.0 / EXPERTS_PER_GROUP))
    el = jnp.where(egroup == g_idx, logits, neg_inf)
    m1 = jnp.max(el, axis=-1, keepdims=True)
    i1 = jnp.min(jnp.where(el == m1, eidx, far), axis=-1, keepdims=True)
    el2 = jnp.where(eidx == i1, neg_inf, el)
    m2 = jnp.max(el2, axis=-1, keepdims=True)
    i2 = jnp.min(jnp.where(el2 == m2, eidx, far), axis=-1, keepdims=True)
    t = jnp.exp(m2 - m1)
    w_a = 1.0 / (1.0 + t)
    e1_ref[...] = i1.astype(I32)
    e2_ref[...] = i2.astype(I32)
    g1_ref[...] = g_w * w_a
    g2_ref[...] = g_w * (t * w_a)

    @pl.when(first_step)
    def _():
        carry[...] = cnt0_ref[...]

    sel1 = eidx == i1
    sel2 = eidx == i2
    onehot = jnp.where(sel1, 1.0, jnp.where(sel2, 1.0, 0.0))
    before = jnp.dot(tri_ref[...], onehot.astype(BF16), preferred_element_type=F32) + carry[...]
    r1_ref[...] = jnp.sum(jnp.where(sel1, before, 0.0), axis=-1, keepdims=True).astype(I32)
    r2_ref[...] = jnp.sum(jnp.where(sel2, before, 0.0), axis=-1, keepdims=True).astype(I32)
    carry[...] = carry[...] + jnp.sum(onehot, axis=0, keepdims=True)
    cnt_ref[...] = carry[...]


def _ffn_out(n, d, tm, row):
    nbp = d // (2 * LANES)
    col = pl.BlockSpec((tm, 1), lambda *g: (row(*g), 0))
    specs = [pl.BlockSpec((tm, d), lambda *g: (row(*g), 0)),
             pl.BlockSpec((tm * nbp, LANES), lambda *g: (row(*g), 0))] + [col] * 6 \
        + [pl.BlockSpec((1, N_ROUTER), lambda *g: (0, 0))]
    shapes = [jax.ShapeDtypeStruct((n, d), F32), jax.ShapeDtypeStruct((n * nbp, LANES), jnp.uint32),
              jax.ShapeDtypeStruct((n, 1), I32), jax.ShapeDtypeStruct((n, 1), I32),
              jax.ShapeDtypeStruct((n, 1), F32), jax.ShapeDtypeStruct((n, 1), F32),
              jax.ShapeDtypeStruct((n, 1), I32), jax.ShapeDtypeStruct((n, 1), I32),
              jax.ShapeDtypeStruct((1, N_ROUTER), F32)]
    return specs, shapes


def _strict_lower(tm):
    r = jnp.arange(tm)
    return (r[:, None] > r[None, :]).astype(BF16)


def _attn_out_kernel(a_ref, x_ref, wo32_ref, gf_ref, wcat_ref, br_ref, tri_ref, cnt0_ref, *rest):
    outs, carry, wo_ref = rest[:-2], rest[-2], rest[-1]

    @pl.when(pl.program_id(0) == 0)
    def _():
        wo_ref[...] = wo32_ref[...].astype(BF16)

    nbatch, npair, rows, _ = a_ref.shape
    a = jnp.concatenate([a_ref[:, hp].reshape(nbatch * rows, LANES) for hp in range(npair)], axis=-1)
    x1 = x_ref[...] + jnp.dot(a, wo_ref[...], preferred_element_type=F32)
    _ffn_prologue(x1, gf_ref, wcat_ref, br_ref, tri_ref, cnt0_ref, pl.program_id(0) == 0, *outs, carry)


def _attn_out(a, x, wo, gf, wcat, br, cnt0, *, tm):
    n, d = x.shape
    b, npair, t, _ = a.shape
    row = lambda i: (i, 0)
    fixed = lambda i: (0, 0)
    tri = _strict_lower(tm)
    specs, shapes = _ffn_out(n, d, tm, lambda i: i)
    if t >= tm:
        a_spec = pl.BlockSpec((1, npair, tm, LANES), lambda i: (i // (t // tm), 0, i % (t // tm), 0))
    else:
        a_spec = pl.BlockSpec((tm // t, npair, t, LANES), lambda i: (i, 0, 0, 0))
    return pl.pallas_call(
        _attn_out_kernel,
        grid=(n // tm,),
        in_specs=[
            a_spec, pl.BlockSpec((tm, d), row),
            pl.BlockSpec(wo.shape, fixed, pipeline_mode=pl.Buffered(1)), pl.BlockSpec(gf.shape, fixed),
            pl.BlockSpec(wcat.shape, fixed), pl.BlockSpec(br.shape, fixed), pl.BlockSpec(tri.shape, fixed),
            pl.BlockSpec(cnt0.shape, fixed),
        ],
        out_specs=specs,
        out_shape=shapes,
        scratch_shapes=[pltpu.VMEM((1, N_ROUTER), F32), pltpu.VMEM(wo.shape, BF16)],
        compiler_params=_params("arbitrary"),
        name="attn_out_router",
    )(a, x, wo, gf, wcat, br, tri, cnt0)


def _conv_kernel(x_ref, st_ref, gm_ref, w1_ref, b1_ref, wdw_ref, bdw_ref, lg_ref, lb_ref, w2_ref, b2_ref,
                 gf_ref, wcat_ref, br_ref, tri_ref, cnt0_ref, *rest, width):
    outs, ns_ref = rest[:9], rest[9]
    ubuf, rbuf, dwbuf, carry = rest[10:]
    bi = pl.program_id(0)
    t = pl.program_id(1)
    tt = x_ref.shape[1]
    d = x_ref.shape[2]
    nb = d // LANES
    state = width - 1
    lo = CONV_HEAD - state
    shifted_rows = rbuf.shape[1]
    rb = min(CONV_ROW_BLOCK, tt)

    @pl.when(t == 0)
    def _():
        for c in range(nb):
            ubuf[c, lo:CONV_HEAD, :] = st_ref[0, :, c * LANES:(c + 1) * LANES]

    x = x_ref[0]
    xn = _rms(x, gm_ref[...]).astype(BF16)

    def glu(p):
        hid = jnp.dot(xn, w1_ref[p], preferred_element_type=F32) + b1_ref[p]
        for k in range(2):
            u = hid[:, k * LANES:(k + 1) * LANES] * jax.nn.sigmoid(hid[:, (2 + k) * LANES:(3 + k) * LANES])
            ubuf[2 * p + k, CONV_HEAD:CONV_HEAD + tt, :] = u
            ns_ref[0, 2 * p + k] = u[tt - state:tt, :]

    def depthwise(c):
        for b in range(1, SUBLANES):
            rbuf[b - 1] = ubuf[c, b:b + shifted_rows, :]
        for r in range(tt // rb):
            r0 = r * rb
            acc = jnp.broadcast_to(bdw_ref[c], (rb, LANES))
            for j in range(width):
                a8, b = divmod(lo + j, SUBLANES)
                if b == 0:
                    src = ubuf[c, lo + j + r0:lo + j + r0 + rb, :]
                else:
                    src = rbuf[b - 1, a8 * SUBLANES + r0:a8 * SUBLANES + r0 + rb, :]
                acc = acc + src * wdw_ref[c, j:j + 1, :]
            dwbuf[c, r0:r0 + rb, :] = acc
        ubuf[c, lo:CONV_HEAD, :] = ubuf[c, lo + tt:CONV_HEAD + tt, :]

    glu(0)
    for p in range(nb // 2):
        if p + 1 < nb // 2:
            glu(p + 1)
        depthwise(2 * p)
        depthwise(2 * p + 1)

    dw = jnp.concatenate([dwbuf[c] for c in range(nb)], axis=-1)
    mu = jnp.mean(dw, axis=-1, keepdims=True)
    xc = dw - mu
    z = xc * lax.rsqrt(jnp.mean(xc * xc, axis=-1, keepdims=True) + LN_EPS) * lg_ref[...] + lb_ref[...]
    z = z * jax.nn.sigmoid(z)
    x1 = x + jnp.dot(z.astype(BF16), w2_ref[...], preferred_element_type=F32) + b2_ref[...]
    first = jnp.logical_and(bi == 0, t == 0)
    _ffn_prologue(x1, gf_ref, wcat_ref, br_ref, tri_ref, cnt0_ref, first, *outs, carry)


def _conv_layer(x, state, gm, w1, b1, wdw, bdw, lg, lb, w2, b2, gf, wcat, br, cnt0, *, tt):
    b, t, d = x.shape
    width = wdw.shape[0]
    nb = d // LANES
    nt = t // tt
    wdw3 = wdw.reshape(width, nb, LANES).transpose(1, 0, 2)
    bdw3 = bdw.reshape(nb, 1, LANES)
    pair_cols = lambda m: jnp.concatenate([m[..., :d].reshape(-1, nb // 2, 2 * LANES),
                                           m[..., d:].reshape(-1, nb // 2, 2 * LANES)], axis=2).transpose(1, 0, 2)
    w1 = pair_cols(w1)
    b1 = pair_cols(b1)
    tri = _strict_lower(tt)
    fixed2 = lambda bi, ti: (0, 0)
    fixed3 = lambda bi, ti: (0, 0, 0)
    specs, shapes = _ffn_out(b * t, d, tt, lambda bi, ti: bi * nt + ti)
    shifted_rows = tt + ((CONV_HEAD - 1) // SUBLANES) * SUBLANES
    return pl.pallas_call(
        functools.partial(_conv_kernel, width=width),
        grid=(b, nt),
        in_specs=[
            pl.BlockSpec((1, tt, d), lambda bi, ti: (bi, ti, 0)),
            pl.BlockSpec((1, width - 1, d), lambda bi, ti: (bi, 0, 0)),
            pl.BlockSpec(gm.shape, fixed2), pl.BlockSpec(w1.shape, fixed3), pl.BlockSpec(b1.shape, fixed3),
            pl.BlockSpec(wdw3.shape, fixed3), pl.BlockSpec(bdw3.shape, fixed3),
            pl.BlockSpec(lg.shape, fixed2), pl.BlockSpec(lb.shape, fixed2),
            pl.BlockSpec(w2.shape, fixed2), pl.BlockSpec(b2.shape, fixed2),
            pl.BlockSpec(gf.shape, fixed2), pl.BlockSpec(wcat.shape, fixed2), pl.BlockSpec(br.shape, fixed2),
            pl.BlockSpec(tri.shape, fixed2), pl.BlockSpec(cnt0.shape, fixed2),
        ],
        out_specs=specs + [pl.BlockSpec((1, nb, width - 1, LANES), lambda bi, ti: (bi, 0, 0, 0))],
        out_shape=shapes + [jax.ShapeDtypeStruct((b, nb, width - 1, LANES), F32)],
        scratch_shapes=[
            pltpu.VMEM((nb, CONV_HEAD + tt, LANES), F32),
            pltpu.VMEM((SUBLANES - 1, shifted_rows, LANES), F32),
            pltpu.VMEM((nb, tt, LANES), F32),
            pltpu.VMEM((1, N_ROUTER), F32),
        ],
        compiler_params=_params("arbitrary", "arbitrary"),
        name="conv_module_router",
    )(x, state, gm, w1, b1, wdw3, bdw3, lg, lb, w2, b2, gf, wcat, br, tri, cnt0)


def _slot(slots_ref, tm, which, r):
    return slots_ref[0, 0, which * tm + r]


def _row_copy(src, src_row, dst, dst_row, sem, nb):
    return pltpu.make_async_copy(src.at[pl.ds(pl.multiple_of(src_row * nb, nb), nb), :],
                                 dst.at[pl.ds(pl.multiple_of(dst_row * nb, nb), nb), :], sem)


def _dispatch_kernel(starts_ref, ends_ref, meta_ref, *rest, tm, moe_tm, nb, group_steps):
    n_groups = len(group_steps)
    xn_refs = rest[:n_groups]
    xs_hbm, zbuf, sem, zsem = rest[n_groups:]
    i = pl.program_id(0)

    @pl.when(i == 0)
    def _():
        zbuf[...] = jnp.zeros_like(zbuf)

        def tile_copy(e):
            return pltpu.make_async_copy(
                zbuf, xs_hbm.at[pl.ds(pl.multiple_of((ends_ref[e] - moe_tm) * nb, nb), moe_tm * nb), :], zsem)

        for e in range(N_EXPERTS):
            @pl.when(ends_ref[e] > starts_ref[e])
            def _():
                tile_copy(e).start()
        for e in range(N_EXPERTS):
            @pl.when(ends_ref[e] > starts_ref[e])
            def _():
                tile_copy(e).wait()

        def unused_tile(j):
            return pltpu.make_async_copy(
                zbuf, xs_hbm.at[pl.ds(pl.multiple_of(j * (moe_tm * nb), moe_tm * nb), moe_tm * nb), :], zsem)

        first_unused = ends_ref[N_EXPERTS - 1] // moe_tm
        n_tiles = xs_hbm.shape[0] // (moe_tm * nb)
        lax.fori_loop(first_unused, n_tiles, lambda j, c: (unused_tile(j).start(), c)[1], 0)
        lax.fori_loop(first_unused, n_tiles, lambda j, c: (unused_tile(j).wait(), c)[1], 0)

    def scatter_tile(xn_ref):
        def copies(r):
            return [_row_copy(xn_ref, r, xs_hbm, _slot(meta_ref, tm, w, r), sem, nb) for w in range(2)]

        def issue(g, carry):
            for k in range(DMA_UNROLL):
                for queue, cp in enumerate(copies(g * DMA_UNROLL + k)):
                    cp.start(priority=queue)
            return carry

        def drain(g, carry):
            for k in range(DMA_UNROLL):
                for cp in copies(g * DMA_UNROLL + k):
                    cp.wait()
            return carry

        lax.fori_loop(0, tm // DMA_UNROLL, issue, 0)
        lax.fori_loop(0, tm // DMA_UNROLL, drain, 0)

    first = 0
    for xn_ref, steps in zip(xn_refs, group_steps):
        pl.when(jnp.logical_and(i >= first, i < first + steps))(functools.partial(scatter_tile, xn_ref))
        first += steps


def _dispatch(starts, ends, meta, xn_rows_list, *, d, tm, moe_tm, p_max):
    nb = d // (2 * LANES)
    group_steps = tuple(x.shape[0] // (nb * tm) for x in xn_rows_list)
    firsts = tuple(sum(group_steps[:k]) for k in range(len(group_steps)))

    def tile_of(k):
        return lambda i, s, e: (jnp.clip(i - firsts[k], 0, group_steps[k] - 1), 0)

    return pl.pallas_call(
        functools.partial(_dispatch_kernel, tm=tm, moe_tm=moe_tm, nb=nb, group_steps=group_steps),
        grid_spec=pltpu.PrefetchScalarGridSpec(
            num_scalar_prefetch=2,
            grid=(sum(group_steps),),
            in_specs=[pl.BlockSpec((1, 1, 2 * tm), lambda i, s, e: (i, 0, 0), memory_space=pltpu.SMEM)]
            + [pl.BlockSpec((tm * nb, LANES), tile_of(k)) for k in range(len(group_steps))],
            out_specs=pl.BlockSpec(memory_space=pl.ANY),
            scratch_shapes=[pltpu.VMEM((moe_tm * nb, LANES), jnp.uint32),
                            pltpu.SemaphoreType.DMA, pltpu.SemaphoreType.DMA],
        ),
        out_shape=jax.ShapeDtypeStruct((p_max * nb, LANES), jnp.uint32),
        compiler_params=_params("arbitrary"),
        name="moe_dispatch",
    )(starts, ends, meta, *xn_rows_list)


def _moe_kernel(te_ref, nv_ref, x_ref, wg_ref, wu_ref, wd_ref, o_ref, wgu, wdn, *, tm, nb):
    i = pl.program_id(0)
    f = wg_ref.shape[3]

    @pl.when(jnp.logical_or(i == 0, te_ref[i] != te_ref[jnp.maximum(i - 1, 0)]))
    def _():
        wgu[:, 0:f] = wg_ref[0, 0].astype(BF16)
        wgu[:, f:2 * f] = wu_ref[0, 0].astype(BF16)
        wdn[...] = wd_ref[0, 0].astype(BF16)

    @pl.when(i < nv_ref[0])
    def _():
        x = _unpack_bf16_pairs(_load_rows(x_ref, tm, nb // 2))
        hu = jnp.dot(x, wgu[...], preferred_element_type=F32)
        h, u = hu[:, 0:f], hu[:, f:2 * f]
        hid = (h * jax.nn.sigmoid(h)) * u
        _store_rows(o_ref, jnp.dot(hid.astype(BF16), wdn[...], preferred_element_type=F32))

    @pl.when(i >= nv_ref[0])
    def _():
        o_ref[...] = jnp.zeros_like(o_ref)


def _moe_grouped(tile_expert, n_valid, xs_rows, wg, wu, wd, *, layer, tm, d):
    nb = d // LANES
    p = xs_rows.shape[0] // (nb // 2)
    f = wg.shape[3]
    tile = lambda i, te, nv: (jnp.minimum(i, nv[0] - 1), 0)
    wmap = lambda i, te, nv: (layer, te[i], 0, 0)
    return pl.pallas_call(
        functools.partial(_moe_kernel, tm=tm, nb=nb),
        grid_spec=pltpu.PrefetchScalarGridSpec(
            num_scalar_prefetch=2,
            grid=(p // tm,),
            in_specs=[
                pl.BlockSpec((tm * nb // 2, LANES), tile),
                pl.BlockSpec((1, 1, d, f), wmap), pl.BlockSpec((1, 1, d, f), wmap),
                pl.BlockSpec((1, 1, f, d), wmap),
            ],
            out_specs=pl.BlockSpec((tm * nb, LANES), lambda i, te, nv: (i, 0)),
            scratch_shapes=[pltpu.VMEM((d, 2 * f), BF16), pltpu.VMEM((f, d), BF16)],
        ),
        out_shape=jax.ShapeDtypeStruct((p * nb, LANES), F32),
        compiler_params=_params("arbitrary"),
        name="moe_grouped_mlp",
    )(tile_expert, n_valid, xs_rows, wg, wu, wd)


def _combine_kernel(meta_ref, meta_next_ref, x1_ref, g1_ref, g2_ref, gfin_ref, ys_hbm, o_ref,
                    buf, sem, *, tm, nb, final_norm):
    i = pl.program_id(0)
    n_steps = pl.num_programs(0)

    def copies(meta, slot, r):
        return [_row_copy(ys_hbm, _slot(meta, tm, w, r), buf.at[slot, w], r, sem.at[slot], nb)
                for w in range(2)]

    def issue(meta, slot):
        def body(g, carry):
            for k in range(DMA_UNROLL):
                for queue, cp in enumerate(copies(meta, slot, g * DMA_UNROLL + k)):
                    cp.start(priority=queue)
            return carry
        lax.fori_loop(0, tm // DMA_UNROLL, body, 0)

    def drain(meta, slot):
        def body(g, carry):
            for k in range(DMA_UNROLL):
                for cp in copies(meta, slot, g * DMA_UNROLL + k):
                    cp.wait()
            return carry
        lax.fori_loop(0, tm // DMA_UNROLL, body, 0)

    slot = jnp.bitwise_and(i, 1)

    @pl.when(i == 0)
    def _():
        issue(meta_ref, 0)

    @pl.when(i + 1 < n_steps)
    def _():
        issue(meta_next_ref, 1 - slot)

    drain(meta_ref, slot)
    y = (x1_ref[...] + g1_ref[...] * _load_rows(buf.at[slot, 0], tm, nb)
         + g2_ref[...] * _load_rows(buf.at[slot, 1], tm, nb))
    o_ref[...] = _rms(y, gfin_ref[...]) if final_norm else y


def _combine(meta, x1, g1, g2, gfin, ys_rows, *, tm, final_norm):
    n, d = x1.shape
    nb = d // LANES
    n_steps = n // tm
    row = lambda i: (i, 0)
    return pl.pallas_call(
        functools.partial(_combine_kernel, tm=tm, nb=nb, final_norm=final_norm),
        grid=(n_steps,),
        in_specs=[
            pl.BlockSpec((1, 1, 2 * tm), lambda i: (i, 0, 0), memory_space=pltpu.SMEM),
            pl.BlockSpec((1, 1, 2 * tm), lambda i: (jnp.minimum(i + 1, n_steps - 1), 0, 0),
                         memory_space=pltpu.SMEM),
            pl.BlockSpec((tm, d), row), pl.BlockSpec((tm, 1), row), pl.BlockSpec((tm, 1), row),
            pl.BlockSpec((1, d), lambda i: (0, 0)),
            pl.BlockSpec(memory_space=pl.ANY),
        ],
        out_specs=pl.BlockSpec((tm, d), row),
        out_shape=jax.ShapeDtypeStruct((n, d), F32),
        scratch_shapes=[pltpu.VMEM((2, 2, tm * nb, LANES), F32), pltpu.SemaphoreType.DMA((2,))],
        compiler_params=_params("arbitrary"),
        name="moe_combine",
    )(meta, meta, x1, g1, g2, gfin, ys_rows)


def _moe(ffns, wg, wu, wd, gfin, *, layer, final_norm):
    d = ffns[0][0].shape[1]
    moe_tm = MOE_ROW_TILE
    n_all = sum(f[0].shape[0] for f in ffns)
    counts = ffns[-1][8][0, N_GROUPS:].astype(I32)
    padded = ((counts + moe_tm - 1) // moe_tm) * moe_tm
    ends = jnp.cumsum(padded)
    starts = ends - padded
    p_max = 2 * n_all + N_EXPERTS * moe_tm
    n_tiles = p_max // moe_tm
    n_valid = ends[-1] // moe_tm
    tile_start = jnp.arange(n_tiles, dtype=I32) * moe_tm
    tile_expert = jnp.sum((tile_start[:, None] >= ends[None, :]).astype(I32), axis=1)
    last_expert = jnp.sum((ends[-1] - 1 >= ends).astype(I32))
    tile_expert = jnp.minimum(tile_expert, last_expert)
    experts = jnp.arange(N_EXPERTS, dtype=I32)[None, :]
    slot_of = lambda e, r: r + jnp.sum(jnp.where(e == experts, starts[None, :], 0), axis=1, keepdims=True)
    tm = math.gcd(MOE_IO_TILE, *[f[0].shape[0] for f in ffns])
    metas = [jnp.concatenate([slot_of(f[2], f[6]).reshape(-1, tm), slot_of(f[3], f[7]).reshape(-1, tm)],
                             axis=1).reshape(-1, 1, 2 * tm) for f in ffns]
    xs_rows = _dispatch(starts, ends, jnp.concatenate(metas, axis=0), [f[1] for f in ffns],
                        d=d, tm=tm, moe_tm=moe_tm, p_max=p_max)
    ys_rows = _moe_grouped(tile_expert, n_valid[None], xs_rows, wg, wu, wd, layer=layer, tm=moe_tm, d=d)
    return [_combine(meta, f[0], f[4], f[5], gfin, ys_rows, tm=tm, final_norm=final_norm)
            for f, meta in zip(ffns, metas)]


def kernel(x_prompt, x_sample, cache_attn_k, cache_attn_v, cache_conv, norm_mix, norm_ffn, norm_final,
           w_qkv, w_o, rel_table, w_pw1, b_pw1, w_dw, b_dw, ln_g, ln_b, w_pw2, b_pw2,
           w_group, b_group, w_router, b_router, w_gate, w_up, w_down):
    bp, tp, d = x_prompt.shape
    bs, ts, _ = x_sample.shape
    depth = norm_mix.shape[0]
    kv_keep = cache_attn_k.shape[2]
    conv_state = cache_conv.shape[2]
    assert kv_keep == BAND_PAST and tp % ROW_TILE == 0 and ts == CHUNK and ROW_TILE == BAND_PAST
    assert w_group.shape[2] == N_GROUPS and w_router.shape[2] == N_EXPERTS
    tm_s = min(ROW_TILE, bs * ts)

    groups = ((bp, tp, ROW_TILE), (bs, ts, tm_s))
    xs_cur = [x_prompt.reshape(bp * tp, d), x_sample.reshape(bs * ts, d)]
    kv_out = [[[], []], [[], []]]
    conv_out = [[], []]
    gfin = norm_final[None, :]

    for layer in range(depth):
        gm = norm_mix[layer][None, :]
        gf = norm_ffn[layer][None, :]
        wr = jnp.concatenate([w_group[layer], w_router[layer]], axis=1)
        wr_hi = wr.astype(BF16)
        wcat = jnp.concatenate([wr_hi, (wr - wr_hi.astype(F32)).astype(BF16)], axis=1)
        br = jnp.concatenate([b_group[layer], b_router[layer]])[None, :]
        ffn = []
        cnt0 = jnp.zeros((1, N_ROUTER), F32)
        if layer % 2 == 0:
            a = layer // 2
            wqkv = w_qkv[a]
            wo = w_o[a]
            for gi, (b, t, tm) in enumerate(groups):
                x3 = xs_cur[gi].reshape(b, t, d)
                if gi == 0:
                    q, kpad, vpad, k32, v32 = _qkv(x3, gm, wqkv, tm=ROW_TILE, pad_blocks=1)
                    bias = _band_bias(rel_table[a], ATTN_Q_ROWS)
                    att = _attention(q, kpad, kpad, vpad, vpad, bias, tb=ROW_TILE, tq=ATTN_Q_ROWS,
                                     invalid_rows=BAND_PAST, cur0=BAND_PAST // ROW_TILE)
                    k_new, v_new = k32, v32
                else:
                    q, kn, vn, k32, v32 = _qkv(x3, gm, wqkv, tm=t, pad_blocks=0)
                    ck = cache_attn_k[a].reshape(b, kv_keep, d)
                    cv = cache_attn_v[a].reshape(b, kv_keep, d)
                    bias = _band_bias(rel_table[a], t)
                    att = _attention(q, ck, kn, cv, vn, bias, tb=t, tq=t, invalid_rows=0, cur0=0)
                    k_new = jnp.concatenate([ck, k32], axis=1)[:, -kv_keep:]
                    v_new = jnp.concatenate([cv, v32], axis=1)[:, -kv_keep:]
                kv_out[gi][0].append(k_new.reshape(b, kv_keep, N_HEADS, d // N_HEADS))
                kv_out[gi][1].append(v_new.reshape(b, kv_keep, N_HEADS, d // N_HEADS))
                ffn.append(_attn_out(att, xs_cur[gi], wo, gf, wcat, br, cnt0, tm=tm))
                cnt0 = ffn[-1][8]
        else:
            c = layer // 2
            w1 = w_pw1[c].astype(BF16)
            w2 = w_pw2[c].astype(BF16)
            for gi, (b, t, tm) in enumerate(groups):
                x3 = xs_cur[gi].reshape(b, t, d)
                state = jnp.zeros((b, conv_state, d), F32) if gi == 0 else cache_conv[c]
                outs = _conv_layer(x3, state, gm, w1, b_pw1[c][None, :], w_dw[c], b_dw[c], ln_g[c][None, :],
                                   ln_b[c][None, :], w2, b_pw2[c][None, :], gf, wcat, br, cnt0,
                                   tt=ROW_TILE if gi == 0 else t)
                ffn.append(outs[:9])
                cnt0 = outs[8]
                conv_out[gi].append(outs[9].transpose(0, 2, 1, 3).reshape(b, conv_state, d))
        last = layer == depth - 1
        xs_cur = _moe(ffn, w_gate, w_up, w_down, gfin, layer=layer, final_norm=last)

    y_prompt = xs_cur[0].reshape(bp, tp, d)
    y_sample = xs_cur[1].reshape(bs, ts, d)
    return (y_prompt, y_sample,
            jnp.stack(kv_out[0][0]), jnp.stack(kv_out[0][1]), jnp.stack(conv_out[0]),
            jnp.stack(kv_out[1][0]), jnp.stack(kv_out[1][1]), jnp.stack(conv_out[1]))
```
